```python
import math
import jax
import jax.numpy as jnp
from jax import lax
import numpy as np

D_MODEL = 1024
BATCH = 16
SEQ = 2048
DEPTH = 2

CTX_LEN = 256
GRID_W = 64
N_MIXERS = 2
D_FF = 4 * D_MODEL
S5_GROUP_CH = 16
S5_GROUPS = D_MODEL // S5_GROUP_CH
S5_STATE = 64
S5_DT_MIN = 1e-3
S5_DT_MAX = 1e-1
GDN_KEY_HEADS = 8
GDN_VALUE_HEADS = 16
GDN_HEAD_DIM = 128
GDN_QK_DIM = GDN_KEY_HEADS * GDN_HEAD_DIM
GDN_V_DIM = GDN_VALUE_HEADS * GDN_HEAD_DIM
GDN_CHUNK = 64
GDN_CONV = 5
GDN_IN_DIM = 2 * GDN_QK_DIM + 2 * GDN_V_DIM + 4 * GDN_VALUE_HEADS
N_S5_LAYERS = (DEPTH + 1) // 2
N_GDN_LAYERS = DEPTH // 2
NORM_EPS = 1e-6

kernel_name = 'hybrid_s5_gated_deltanet_dit'


def _rmsnorm(t, w):
    t32 = t.astype(jnp.float32)
    y = t32 * lax.rsqrt(jnp.mean(jnp.square(t32), axis=-1, keepdims=True) + NORM_EPS)
    return (y * w.astype(jnp.float32)).astype(t.dtype)


def _modulate(h, shift, scale):
    return h * (1 + scale) + shift


def _sqrelu_mlp(h, w1, w2):
    return jnp.square(jax.nn.relu(h @ w1)) @ w2


def _depthwise_conv(t, w):
    k = w.shape[0]
    return lax.conv_general_dilated(
        t, w[:, None, :].astype(t.dtype), window_strides=(1,),
        padding=[(k // 2, k // 2)], dimension_numbers=('NWC', 'WIO', 'NWC'),
        feature_group_count=t.shape[-1])


def _l2norm(t):
    t32 = t.astype(jnp.float32)
    return t32 * lax.rsqrt(jnp.sum(jnp.square(t32), axis=-1, keepdims=True) + NORM_EPS)


def _linear_recurrence(left, right):
    a_i, b_i = left
    a_j, b_j = right
    return a_j * a_i, a_j * b_i + b_j


def _s5_direction(u, a_bar, b_bar, c_mat, h0, reverse, need_out):
    bu = jnp.einsum('blgh,gph->lbgp', u.astype(jnp.complex64), b_bar)
    if reverse:
        bu = jnp.flip(bu, axis=0)
    if h0 is not None:
        bu = bu.at[0].add(a_bar * h0)
    a = jnp.broadcast_to(a_bar, (bu.shape[0], 1) + a_bar.shape)
    _, h = lax.associative_scan(_linear_recurrence, (a, bu), axis=0)
    final = h[-1]
    if not need_out:
        return None, final
    y = jnp.einsum('lbgp,ghp->blgh', h, c_mat).real
    if reverse:
        y = jnp.flip(y, axis=1)
    return y, final


def _s5_mixer(h_ctx, h_lat, lam_re, lam_im, log_dt, b_re, b_im, c_re, c_im, d_skip, w_glu, need_ctx_out):
    f32 = jnp.float32
    lam = lax.complex(lam_re.astype(f32), lam_im.astype(f32))
    dt = jnp.exp(log_dt.astype(f32))[..., None]
    a_bar = jnp.exp(lam * dt)
    b_bar = ((a_bar - 1) / lam)[..., None] * lax.complex(b_re.astype(f32), b_im.astype(f32))
    c_mat = lax.complex(c_re.astype(f32), c_im.astype(f32))

    def groups(t):
        return t.astype(f32).reshape(t.shape[0], t.shape[1], S5_GROUPS, S5_GROUP_CH)

    u_c, u_l = groups(h_ctx), groups(h_lat)
    yc_f, st_f = _s5_direction(u_c, a_bar[0], b_bar[0], c_mat[0], None, False, need_ctx_out)
    yc_b, st_b = _s5_direction(u_c, a_bar[1], b_bar[1], c_mat[1], None, True, need_ctx_out)
    yl_f, _ = _s5_direction(u_l, a_bar[0], b_bar[0], c_mat[0], st_f, False, True)
    yl_b, _ = _s5_direction(u_l, a_bar[1], b_bar[1], c_mat[1], st_b, True, True)

    def post(y, u_in):
        y = y.reshape(u_in.shape) + d_skip.astype(f32) * u_in.astype(f32)
        y = jax.nn.gelu(y).astype(u_in.dtype)
        y_a, y_b = jnp.split(y @ w_glu, 2, axis=-1)
        return y_a * jax.nn.sigmoid(y_b)

    out_lat = post(yl_f + yl_b, h_lat)
    out_ctx = post(yc_f + yc_b, h_ctx) if need_ctx_out else None
    return out_ctx, out_lat


def _qk_heads(t):
    b, l, _ = t.shape
    t = _l2norm(t.reshape(b, l, GDN_KEY_HEADS, GDN_HEAD_DIM))
    return jnp.repeat(t, GDN_VALUE_HEADS // GDN_KEY_HEADS, axis=2)


def _gated_delta_chunked(q, k, v, beta, g, state0, with_output):
    out_dtype = v.dtype
    bsz, seq, nh, _ = k.shape
    nc = seq // GDN_CHUNK

    def blocks(t):
        t = t.astype(jnp.float32).reshape((bsz, nc, GDN_CHUNK, nh) + t.shape[3:])
        return jnp.moveaxis(jnp.moveaxis(t, 3, 2), 1, 0)

    k_b, v_b, beta_b = blocks(k), blocks(v), blocks(beta)
    g_b = jnp.cumsum(blocks(g), axis=-1)
    incl = jnp.tril(jnp.ones((GDN_CHUNK, GDN_CHUNK), dtype=bool))
    strict = jnp.tril(jnp.ones((GDN_CHUNK, GDN_CHUNK), dtype=bool), k=-1)
    decay = jnp.exp(jnp.where(incl, g_b[..., :, None] - g_b[..., None, :], -jnp.inf))
    k_beta = k_b * beta_b[..., None]
    lower = jnp.where(strict, jnp.einsum('nbhcd,nbhsd->nbhcs', k_beta, k_b) * decay, 0.0)
    eye = jnp.eye(GDN_CHUNK, dtype=jnp.float32)
    t_mat = lax.linalg.triangular_solve(lower + eye, jnp.broadcast_to(eye, lower.shape),
                                        left_side=True, lower=True, unit_diagonal=True)
    u = jnp.einsum('nbhcs,nbhse->nbhce', t_mat, v_b * beta_b[..., None])
    w = jnp.einsum('nbhcs,nbhsd->nbhcd', t_mat, k_beta * jnp.exp(g_b)[..., None])
    g_last = g_b[..., -1]
    k_end = k_b * jnp.exp(g_last[..., None] - g_b)[..., None]
    state0 = state0.astype(jnp.float32)

    def advance(state, w_i, u_i, kend_i, glast_i):
        v_new = u_i - jnp.einsum('bhcd,bhde->bhce', w_i, state)
        new_state = state * jnp.exp(glast_i)[..., None, None] + jnp.einsum('bhcd,bhce->bhde', kend_i, v_new)
        return v_new, new_state

    if not with_output:
        def step_state(state, xs):
            _, new_state = advance(state, *xs)
            return new_state, None
        state, _ = lax.scan(step_state, state0, (w, u, k_end, g_last))
        return None, state

    q_b = blocks(q)
    q_g = q_b * jnp.exp(g_b)[..., None]
    intra = jnp.where(incl, jnp.einsum('nbhcd,nbhsd->nbhcs', q_b, k_b) * decay, 0.0)

    def step(state, xs):
        w_i, u_i, kend_i, glast_i, qg_i, intra_i = xs
        v_new, new_state = advance(state, w_i, u_i, kend_i, glast_i)
        o_i = jnp.einsum('bhcd,bhde->bhce', qg_i, state) + jnp.einsum('bhcs,bhse->bhce', intra_i, v_new)
        return new_state, o_i

    state, o = lax.scan(step, state0, (w, u, k_end, g_last, q_g, intra))
    o = jnp.moveaxis(jnp.moveaxis(o, 0, 1), 2, 3).reshape(bsz, seq, nh, -1)
    return o.astype(out_dtype), state


def _flip_seq(t):
    return None if t is None else jnp.flip(t, axis=1)


def _gdn_bidirectional(q, k, v, beta, g, s_f, s_b, with_out):
    o_f, s_f = _gated_delta_chunked(q, k, v, beta[:, :, 0], g[:, :, 0], s_f, with_out)
    o_b, s_b = _gated_delta_chunked(_flip_seq(q), _flip_seq(k), _flip_seq(v),
                                    _flip_seq(beta[:, :, 1]), _flip_seq(g[:, :, 1]), s_b, with_out)
    o = o_f + _flip_seq(o_b) if with_out else None
    return o, s_f, s_b


def _gdn_mixer(h_ctx, h_lat, w_in, conv_w, a_log, dt_bias, onorm_w, w_out, need_ctx_out):
    f32 = jnp.float32
    bsz, seq, _ = h_lat.shape
    rows = seq // GRID_W
    qk, vd = GDN_QK_DIM, GDN_V_DIM

    def conv_lat(t, w):
        y = _depthwise_conv(t.reshape(bsz * rows, GRID_W, t.shape[-1]), w)
        return jax.nn.silu(y.reshape(t.shape))

    def conv_ctx(t, w):
        return jax.nn.silu(_depthwise_conv(t, w))

    def queries(q_raw, conv):
        return _qk_heads(conv(q_raw, conv_w[:, :qk])) * GDN_HEAD_DIM ** -0.5

    def keys_values_gates(rest, conv):
        b, l, _ = rest.shape
        kv = conv(rest[..., :qk + vd], conv_w[:, qk:])
        k = _qk_heads(kv[..., :qk])
        v = kv[..., qk:].reshape(b, l, GDN_VALUE_HEADS, GDN_HEAD_DIM)
        gl = rest[..., qk + vd:].astype(f32).reshape(b, l, 2, 2, GDN_VALUE_HEADS)
        beta = jax.nn.sigmoid(gl[:, :, 0])
        g = -jnp.exp(a_log.astype(f32)) * jax.nn.softplus(gl[:, :, 1] + dt_bias.astype(f32))
        return k, v, beta, g

    def readout(o, z):
        b, l = z.shape[:2]
        o32 = o.astype(f32)
        o32 = o32 * lax.rsqrt(jnp.mean(jnp.square(o32), axis=-1, keepdims=True) + NORM_EPS) * onorm_w.astype(f32)
        gated = o32 * jax.nn.silu(z.astype(f32)).reshape(o.shape)
        return gated.reshape(b, l, vd).astype(z.dtype) @ w_out

    zero_state = jnp.zeros((bsz, GDN_VALUE_HEADS, GDN_HEAD_DIM, GDN_HEAD_DIM), f32)
    if need_ctx_out:
        proj_c = h_ctx @ w_in
        q_c = queries(proj_c[..., :qk], conv_ctx)
        z_c = proj_c[..., qk:qk + vd]
        rest_c = proj_c[..., qk + vd:]
    else:
        q_c, z_c = None, None
        rest_c = h_ctx @ w_in[:, qk + vd:]
    k_c, v_c, beta_c, g_c = keys_values_gates(rest_c, conv_ctx)
    o_c, s_f, s_b = _gdn_bidirectional(q_c, k_c, v_c, beta_c, g_c, zero_state, zero_state, need_ctx_out)
    proj_l = h_lat @ w_in
    q_l = queries(proj_l[..., :qk], conv_lat)
    z_l = proj_l[..., qk:qk + vd]
    k_l, v_l, beta_l, g_l = keys_values_gates(proj_l[..., qk + vd:], conv_lat)
    o_l, _, _ = _gdn_bidirectional(q_l, k_l, v_l, beta_l, g_l, s_f, s_b, True)
    out_lat = readout(o_l, z_l)
    out_ctx = readout(o_c, z_c) if need_ctx_out else None
    return out_ctx, out_lat


def setup_inputs(seed: int = 0) -> dict:
    key = jax.random.key(seed)
    ks = jax.random.split(key, 32)
    f32 = jnp.float32

    def nrm(k, shape, scale):
        return jax.random.normal(k, shape, f32) * scale

    ns, ng = N_S5_LAYERS, N_GDN_LAYERS
    x = nrm(ks[0], (BATCH, SEQ, D_MODEL), 1.0)
    c = nrm(ks[1], (BATCH, D_MODEL), 1.0)
    ctx = nrm(ks[2], (BATCH, CTX_LEN, D_MODEL), 1.0)
    c_ctx = nrm(ks[3], (D_MODEL,), 1.0)
    ada_w = nrm(ks[4], (DEPTH, D_MODEL, 6 * D_MODEL), 0.5 * D_MODEL ** -0.5)
    ada_b = nrm(ks[5], (DEPTH, 6 * D_MODEL), 0.02)
    norm1_w = 1.0 + nrm(ks[6], (DEPTH, D_MODEL), 0.05)
    norm2_w = 1.0 + nrm(ks[7], (DEPTH, D_MODEL), 0.05)
    mlp_w1 = nrm(ks[8], (DEPTH, D_MODEL, D_FF), D_MODEL ** -0.5)
    mlp_w2 = nrm(ks[9], (DEPTH, D_FF, D_MODEL), D_FF ** -0.5)
    s5_lam_re = -0.5 * jnp.exp(nrm(ks[10], (ns, 2, S5_GROUPS, S5_STATE), 0.05))
    s5_lam_im = jnp.broadcast_to(math.pi * jnp.arange(S5_STATE, dtype=f32), (ns, 2, S5_GROUPS, S5_STATE))
    s5_log_dt = jax.random.uniform(ks[11], (ns, 2, S5_GROUPS), f32, math.log(S5_DT_MIN), math.log(S5_DT_MAX))
    s5_b_re = nrm(ks[12], (ns, 2, S5_GROUPS, S5_STATE, S5_GROUP_CH), 0.5 ** 0.5)
    s5_b_im = nrm(ks[13], (ns, 2, S5_GROUPS, S5_STATE, S5_GROUP_CH), 0.5 ** 0.5)
    s5_c_re = nrm(ks[14], (ns, 2, S5_GROUPS, S5_GROUP_CH, S5_STATE), (2 * S5_STATE) ** -0.5)
    s5_c_im = nrm(ks[15], (ns, 2, S5_GROUPS, S5_GROUP_CH, S5_STATE), (2 * S5_STATE) ** -0.5)
    s5_d = nrm(ks[16], (ns, D_MODEL), 1.0)
    s5_w_glu = nrm(ks[17], (ns, D_MODEL, 2 * D_MODEL), D_MODEL ** -0.5)
    gdn_w_in = nrm(ks[18], (ng, D_MODEL, GDN_IN_DIM), D_MODEL ** -0.5)
    gdn_conv_w = nrm(ks[19], (ng, GDN_CONV, 2 * GDN_QK_DIM + GDN_V_DIM), GDN_CONV ** -0.5)
    gdn_a_log = jnp.log(jax.random.uniform(ks[20], (ng, 2, GDN_VALUE_HEADS), f32, 1.0, 16.0))
    dt0 = jnp.exp(jax.random.uniform(ks[21], (ng, 2, GDN_VALUE_HEADS), f32, math.log(1e-3), math.log(1e-1)))
    gdn_dt_bias = dt0 + jnp.log(-jnp.expm1(-dt0))
    gdn_onorm_w = 1.0 + nrm(ks[22], (ng, GDN_HEAD_DIM), 0.05)
    gdn_w_out = nrm(ks[23], (ng, GDN_V_DIM, D_MODEL), GDN_V_DIM ** -0.5)
    final_norm_w = 1.0 + nrm(ks[24], (D_MODEL,), 0.05)
    return {
        'x': x, 'c': c, 'ctx': ctx, 'c_ctx': c_ctx,
        'ada_w': ada_w, 'ada_b': ada_b, 'norm1_w': norm1_w, 'norm2_w': norm2_w,
        'mlp_w1': mlp_w1, 'mlp_w2': mlp_w2,
        's5_lam_re': s5_lam_re, 's5_lam_im': s5_lam_im, 's5_log_dt': s5_log_dt,
        's5_b_re': s5_b_re, 's5_b_im': s5_b_im, 's5_c_re': s5_c_re, 's5_c_im': s5_c_im,
        's5_d': s5_d, 's5_w_glu': s5_w_glu,
        'gdn_w_in': gdn_w_in, 'gdn_conv_w': gdn_conv_w, 'gdn_a_log': gdn_a_log,
        'gdn_dt_bias': gdn_dt_bias, 'gdn_onorm_w': gdn_onorm_w, 'gdn_w_out': gdn_w_out,
        'final_norm_w': final_norm_w,
    }


def reference(x, c, ctx, c_ctx, ada_w, ada_b, norm1_w, norm2_w, mlp_w1, mlp_w2,
              s5_lam_re, s5_lam_im, s5_log_dt, s5_b_re, s5_b_im, s5_c_re, s5_c_im, s5_d, s5_w_glu,
              gdn_w_in, gdn_conv_w, gdn_a_log, gdn_dt_bias, gdn_onorm_w, gdn_w_out, final_norm_w):
    silu_c = jax.nn.silu(c)
    silu_cc = jax.nn.silu(c_ctx)
    for i in range(DEPTH):
        last = i == DEPTH - 1
        j = i // N_MIXERS
        mod_l = jnp.split((silu_c @ ada_w[i] + ada_b[i])[:, None, :], 6, axis=-1)
        mod_c = jnp.split(silu_cc @ ada_w[i] + ada_b[i], 6, axis=-1)
        h_l = _modulate(_rmsnorm(x, norm1_w[i]), mod_l[0], mod_l[1])
        h_c = _modulate(_rmsnorm(ctx, norm1_w[i]), mod_c[0], mod_c[1])
        if i % N_MIXERS == 0:
            m_c, m_l = _s5_mixer(h_c, h_l, s5_lam_re[j], s5_lam_im[j], s5_log_dt[j], s5_b_re[j], s5_b_im[j],
                                 s5_c_re[j], s5_c_im[j], s5_d[j], s5_w_glu[j], not last)
        else:
            m_c, m_l = _gdn_mixer(h_c, h_l, gdn_w_in[j], gdn_conv_w[j], gdn_a_log[j], gdn_dt_bias[j],
                                  gdn_onorm_w[j], gdn_w_out[j], not last)
        x = x + mod_l[2] * m_l
        x = x + mod_l[5] * _sqrelu_mlp(_modulate(_rmsnorm(x, norm2_w[i]), mod_l[3], mod_l[4]), mlp_w1[i], mlp_w2[i])
        if not last:
            ctx = ctx + mod_c[2] * m_c
            ctx = ctx + mod_c[5] * _sqrelu_mlp(_modulate(_rmsnorm(ctx, norm2_w[i]), mod_c[3], mod_c[4]), mlp_w1[i], mlp_w2[i])
    return _rmsnorm(x, final_norm_w)
```

```python
import functools
import math

import jax
import jax.numpy as jnp
import numpy as np
from jax import lax
from jax.experimental import pallas as pl
from jax.experimental.pallas import tpu as pltpu

F32 = jnp.float32
BF16 = jnp.bfloat16
HIGHEST = lax.Precision.HIGHEST

NORM_EPS = 1e-6
GRID_W = 64
GDN_CHUNK = 64
S5_CHUNK = 16
LANES = 128
VMEM_LIMIT = 56 * 1024 * 1024


def _dot(a, b, precision=None):
    return jnp.dot(a, b, preferred_element_type=F32, precision=precision)


def _dot_nt(a, b):
    return lax.dot_general(a, b, (((1,), (1,)), ((), ())), preferred_element_type=F32)


def _dot_tn(a, b):
    return lax.dot_general(a, b, (((0,), (0,)), ((), ())), preferred_element_type=F32)


def _floor_pow2(v, n):
    assert n & (n - 1) == 0
    return jnp.bitwise_and(v, -n)


def _silu(t):
    return t * jax.nn.sigmoid(t)


def _norm_mod(x, nw, shift, scale):
    y = x * lax.rsqrt(jnp.mean(x * x, axis=-1, keepdims=True) + NORM_EPS) * nw
    return y * (1.0 + scale) + shift


def _params(*sem):
    return pltpu.CompilerParams(dimension_semantics=sem, vmem_limit_bytes=VMEM_LIMIT)


def _ada_kernel(c_ref, w_ref, b_ref, o_ref):
    o_ref[0] = _dot(_silu(c_ref[...]), w_ref[0], HIGHEST) + b_ref[0]


def _ada_mod(cvec, ada_w, ada_b):
    depth, d, n = ada_w.shape
    rows = cvec.shape[0]
    tn = n // 4
    return pl.pallas_call(
        _ada_kernel,
        grid=(depth, n // tn),
        in_specs=[pl.BlockSpec((rows, d), lambda i, j: (0, 0)),
                  pl.BlockSpec((1, d, tn), lambda i, j: (i, 0, j)),
                  pl.BlockSpec((1, 1, tn), lambda i, j: (i, 0, j))],
        out_specs=pl.BlockSpec((1, rows, tn), lambda i, j: (i, 0, j)),
        out_shape=jax.ShapeDtypeStruct((depth, rows, n), F32),
        compiler_params=_params("arbitrary", "arbitrary"),
        name="ada_mod",
    )(cvec, ada_w, ada_b.reshape(depth, 1, n))


def _prenorm_kernel(x_ref, nw_ref, mod_ref, o_ref):
    h = _norm_mod(x_ref[...], nw_ref[...], mod_ref[0, 0:1, :], mod_ref[0, 1:2, :])
    o_ref[...] = h.astype(o_ref.dtype)


def _prenorm(x, nw, mod, tile, tiles_per_mod):
    n, d = x.shape
    return pl.pallas_call(
        _prenorm_kernel,
        grid=(n // tile,),
        in_specs=[pl.BlockSpec((tile, d), lambda i: (i, 0)),
                  pl.BlockSpec((1, d), lambda i: (0, 0)),
                  pl.BlockSpec((1, 6, d), lambda i: (i // tiles_per_mod, 0, 0))],
        out_specs=pl.BlockSpec((tile, d), lambda i: (i, 0)),
        out_shape=jax.ShapeDtypeStruct((n, d), BF16),
        compiler_params=_params("parallel"),
        name="prenorm",
    )(x, nw.reshape(1, d), mod)


def _s5_tables(lam_re, lam_im, log_dt, b_re, b_im, c_re, c_im, d_skip, t):
    _, g, p = lam_re.shape
    h = b_re.shape[-1]
    f = lambda a: a.astype(F32)
    lam_re, lam_im, b_re, b_im, c_re, c_im = map(f, (lam_re, lam_im, b_re, b_im, c_re, c_im))
    dt = jnp.exp(f(log_dt))[..., None]
    zr, zi = lam_re * dt, lam_im * dt
    n = jnp.arange(t + 1, dtype=F32)[:, None, None, None]
    mag = jnp.exp(n * zr)
    pr, pi = mag * jnp.cos(n * zi), mag * jnp.sin(n * zi)
    ar, ai = pr[1], pi[1]
    den = lam_re * lam_re + lam_im * lam_im
    fr = ((ar - 1.0) * lam_re + ai * lam_im) / den
    fi = (ai * lam_re - (ar - 1.0) * lam_im) / den
    bbr = fr[..., None] * b_re - fi[..., None] * b_im
    bbi = fr[..., None] * b_im + fi[..., None] * b_re
    car = c_re[None] * pr[:, :, :, None, :] - c_im[None] * pi[:, :, :, None, :]
    cai = c_re[None] * pi[:, :, :, None, :] + c_im[None] * pr[:, :, :, None, :]
    kern = (jnp.einsum('ndgop,dgpi->ndgoi', car, bbr, precision=HIGHEST)
            - jnp.einsum('ndgop,dgpi->ndgoi', cai, bbi, precision=HIGHEST))
    ti = jnp.arange(t)
    lag = ti[None, :] - ti[:, None]
    kf = jnp.where((lag >= 0)[:, :, None, None, None], kern[jnp.clip(lag, 0, t), 0], 0.0)
    kb = jnp.where((lag <= 0)[:, :, None, None, None], kern[jnp.clip(-lag, 0, t), 1], 0.0)
    intra = jnp.transpose(kf + kb, (2, 0, 4, 1, 3)).reshape(g, t * h, t * h)

    def inject(d, powers):
        er, ei = pr[powers, d], pi[powers, d]
        re = er[..., None] * bbr[d][None] - ei[..., None] * bbi[d][None]
        im = er[..., None] * bbi[d][None] + ei[..., None] * bbr[d][None]
        tr = lambda a: jnp.transpose(a, (1, 0, 3, 2)).reshape(g, t * h, p)
        return tr(re), tr(im)

    pad = lambda a: jnp.pad(a, ((0, 0), (0, 0), (0, LANES - p)))
    in_f = inject(0, t - 1 - ti)
    in_b = inject(1, ti)
    w1 = jnp.concatenate([intra] + [pad(a) for a in (*in_f, *in_b)], axis=-1)

    def readout(d, powers):
        tr = lambda a: jnp.transpose(a, (1, 3, 0, 2)).reshape(g, p, t * h)
        return tr(car[powers, d]), tr(-cai[powers, d])

    padr = lambda a: jnp.pad(a, ((0, 0), (0, LANES - p), (0, 0)))
    w2 = jnp.concatenate([padr(a) for a in (*readout(0, ti + 1), *readout(1, t - ti))], axis=1)
    dec = jnp.stack([pr[t, 0], pi[t, 0], pr[t, 1], pi[t, 1]], axis=1)
    dec = jnp.pad(dec, ((0, 0), (0, 4), (0, LANES - p)))
    dsk = jnp.tile(f(d_skip).reshape(g, 1, h), (1, t, 1)).reshape(g, 1, t * h)
    return w1.astype(BF16), w2.astype(BF16), dec, dsk


def _s5_kernel(uc_ref, ul_ref, w1_ref, w2_ref, dec_ref, dsk_ref, yc_ref, yl_ref, z_scr, h_scr,
               *, bsz, ncc, ncl, width):
    rc, rl = ncc * bsz, ncl * bsz
    w1 = w1_ref[0]
    z_scr[0:rc, :] = _dot(uc_ref[0], w1)
    z_scr[rc:rc + rl, :] = _dot(ul_ref[0], w1)
    dec = dec_ref[0]
    zero = jnp.zeros((bsz, LANES), F32)
    cols = [width + k * LANES for k in range(5)]

    def make_step(are, aim, col, hcol):
        def step(row, carry):
            re, im = carry
            r = pl.multiple_of(row, bsz)
            h_scr[pl.ds(r, bsz), hcol:hcol + LANES] = re
            h_scr[pl.ds(r, bsz), hcol + LANES:hcol + 2 * LANES] = im
            s_re = z_scr[pl.ds(r, bsz), cols[col]:cols[col + 1]]
            s_im = z_scr[pl.ds(r, bsz), cols[col + 1]:cols[col + 2]]
            return are * re - aim * im + s_re, are * im + aim * re + s_im
        return step

    fstep = make_step(dec[0:1, :], dec[1:2, :], 0, 0)
    bstep = make_step(dec[2:3, :], dec[3:4, :], 2, 2 * LANES)
    lax.fori_loop(0, ncc + ncl, lambda c, s: fstep(c * bsz, s), (zero, zero))
    st = lax.fori_loop(0, ncc, lambda i, s: bstep((ncc - 1 - i) * bsz, s), (zero, zero))
    lax.fori_loop(0, ncl, lambda i, s: bstep((ncc + ncl - 1 - i) * bsz, s), st)

    w2 = w2_ref[0]
    dsk = dsk_ref[0]
    blk = min(256, rc)
    for u_ref, y_ref, base, rows in ((uc_ref, yc_ref, 0, rc), (ul_ref, yl_ref, rc, rl)):
        for r0 in range(0, rows, blk):
            sl = slice(base + r0, base + r0 + blk)
            y = z_scr[sl, 0:width] + _dot(h_scr[sl, :].astype(BF16), w2)
            y = y + dsk * u_ref[0, r0:r0 + blk, :].astype(F32)
            y_ref[0, r0:r0 + blk, :] = jax.nn.gelu(y).astype(BF16)


def _s5_core(uc, ul, tables, bsz):
    w1, w2, dec, dsk = tables
    g, rc, width = uc.shape
    rl = ul.shape[1]
    kern = functools.partial(_s5_kernel, bsz=bsz, ncc=rc // bsz, ncl=rl // bsz, width=width)
    gmap = lambda i: (i, 0, 0)
    return pl.pallas_call(
        kern,
        grid=(g,),
        in_specs=[pl.BlockSpec((1, rc, width), gmap), pl.BlockSpec((1, rl, width), gmap),
                  pl.BlockSpec((1,) + w1.shape[1:], gmap), pl.BlockSpec((1,) + w2.shape[1:], gmap),
                  pl.BlockSpec((1,) + dec.shape[1:], gmap), pl.BlockSpec((1,) + dsk.shape[1:], gmap)],
        out_specs=[pl.BlockSpec((1, rc, width), gmap), pl.BlockSpec((1, rl, width), gmap)],
        out_shape=[jax.ShapeDtypeStruct(uc.shape, BF16), jax.ShapeDtypeStruct(ul.shape, BF16)],
        scratch_shapes=[pltpu.VMEM((rc + rl, w1.shape[2]), F32), pltpu.VMEM((rc + rl, 4 * LANES), F32)],
        compiler_params=_params("parallel"),
        name="s5_core",
    )(uc, ul, w1, w2, dec, dsk)


def _to_groups(hm, bsz, groups, t):
    n, d = hm.shape
    l = n // bsz
    hch = d // groups
    a = hm.reshape(bsz, l // t, t, groups, hch)
    return jnp.transpose(a, (3, 1, 0, 2, 4)).reshape(groups, (l // t) * bsz, t * hch)


def _from_groups(y, bsz, t):
    g, rows, width = y.shape
    nc = rows // bsz
    hch = width // t
    a = y.reshape(g, nc, bsz, t, hch)
    return jnp.transpose(a, (2, 1, 3, 0, 4)).reshape(bsz * nc * t, g * hch)


def _mixout_kernel(y_ref, w_ref, x_ref, mod_ref, o_ref, *, glu):
    r = _dot(y_ref[...], w_ref[...])
    if glu:
        half = r.shape[-1] // 2
        r = r[:, :half] * jax.nn.sigmoid(r[:, half:])
    o_ref[...] = x_ref[...] + mod_ref[0, 2:3, :] * r


def _mixout(y, w, x, mod, tile, tiles_per_mod, glu):
    n, d = x.shape
    k, nn = w.shape
    return pl.pallas_call(
        functools.partial(_mixout_kernel, glu=glu),
        grid=(n // tile,),
        in_specs=[pl.BlockSpec((tile, k), lambda i: (i, 0)),
                  pl.BlockSpec((k, nn), lambda i: (0, 0)),
                  pl.BlockSpec((tile, d), lambda i: (i, 0)),
                  pl.BlockSpec((1, 6, d), lambda i: (i // tiles_per_mod, 0, 0))],
        out_specs=pl.BlockSpec((tile, d), lambda i: (i, 0)),
        out_shape=jax.ShapeDtypeStruct((n, d), F32),
        compiler_params=_params("parallel"),
        name="mixer_out",
    )(y, w, x, mod)


def _mlp_kernel(x_ref, nw_ref, mod_ref, w1_ref, w2_ref, fw_ref, o_ref, *, ff_blk, final):
    x = x_ref[...]
    h = _norm_mod(x, nw_ref[...], mod_ref[0, 3:4, :], mod_ref[0, 4:5, :]).astype(BF16)
    acc = jnp.zeros(x.shape, F32)
    for f0 in range(0, w1_ref.shape[1], ff_blk):
        a = jnp.maximum(_dot(h, w1_ref[:, f0:f0 + ff_blk]), 0.0)
        acc = acc + _dot((a * a).astype(BF16), w2_ref[f0:f0 + ff_blk, :])
    y = x + mod_ref[0, 5:6, :] * acc
    if final:
        y = y * lax.rsqrt(jnp.mean(y * y, axis=-1, keepdims=True) + NORM_EPS) * fw_ref[...]
    o_ref[...] = y


def _mlp(x, nw, mod, w1, w2, fw, tile, tiles_per_mod, final):
    n, d = x.shape
    ff = w1.shape[1]
    return pl.pallas_call(
        functools.partial(_mlp_kernel, ff_blk=min(1024, ff), final=final),
        grid=(n // tile,),
        in_specs=[pl.BlockSpec((tile, d), lambda i: (i, 0)),
                  pl.BlockSpec((1, d), lambda i: (0, 0)),
                  pl.BlockSpec((1, 6, d), lambda i: (i // tiles_per_mod, 0, 0)),
                  pl.BlockSpec((d, ff), lambda i: (0, 0)),
                  pl.BlockSpec((ff, d), lambda i: (0, 0)),
                  pl.BlockSpec((1, d), lambda i: (0, 0))],
        out_specs=pl.BlockSpec((tile, d), lambda i: (i, 0)),
        out_shape=jax.ShapeDtypeStruct((n, d), F32),
        compiler_params=_params("parallel"),
        name="mlp",
    )(x, nw.reshape(1, d), mod, w1, w2, fw.reshape(1, d))


def _gdn_in_kernel(x_ref, nw_ref, mod_ref, w_ref, wg_ref, cw_ref, gp_ref, p_ref, g_ref, gt_ref, h_scr,
                   *, period, j0, nq, nz, nk, hv, hdim):
    j = pl.program_id(1) + j0
    tb, tn = p_ref.shape

    @pl.when(pl.program_id(1) == 0)
    def _():
        h = _norm_mod(x_ref[...], nw_ref[...], mod_ref[0, 0:1, :], mod_ref[0, 1:2, :])
        h_scr[...] = h.astype(BF16)
        gl = _dot(h, wg_ref[...], HIGHEST)
        lane = lax.broadcasted_iota(jnp.int32, (1, LANES), 1)
        beta = jax.nn.sigmoid(gl)
        t = gl + gp_ref[1:2, :]
        g = gp_ref[0:1, :] * (jnp.maximum(t, 0.0) + jnp.log1p(jnp.exp(-jnp.abs(t))))
        r = lax.broadcasted_iota(jnp.int32, (tb, tb), 0)
        c = lax.broadcasted_iota(jnp.int32, (tb, tb), 1)
        same = _floor_pow2(r, GDN_CHUNK) == _floor_pow2(c, GDN_CHUNK)
        gcf = _dot(jnp.where(same & (c <= r), 1.0, 0.0), g, HIGHEST)
        gcb = _dot(jnp.where(same & (c >= r), 1.0, 0.0), g, HIGHEST)
        out = jnp.where(lane < 2 * hv, beta, jnp.where(lane < 3 * hv, gcf, jnp.where(lane < 4 * hv, gcb, 0.0)))
        g_ref[...] = out
        for cc in range(tb // GDN_CHUNK):
            gt_ref[cc] = out[cc * GDN_CHUNK:(cc + 1) * GDN_CHUNK, :].T

    acc = _dot(h_scr[...], w_ref[...])

    def conv_silu(a):
        row = lax.broadcasted_iota(jnp.int32, (tb, 1), 0)
        pos = row - _floor_pow2(row, period)
        cw = cw_ref[...]
        y = a * cw[2:3, :]
        for tap in (0, 1, 3, 4):
            d = tap - 2
            sh = pltpu.roll(a, (tb - d) % tb, 0)
            ok = (pos + d >= 0) & (pos + d < period)
            y = y + jnp.where(ok, sh, 0.0) * cw[tap:tap + 1, :]
        return _silu(y)

    def l2n(y, scale):
        outs = []
        for h0 in range(0, tn, hdim):
            yh = y[:, h0:h0 + hdim]
            outs.append(yh * (lax.rsqrt(jnp.sum(yh * yh, axis=-1, keepdims=True) + NORM_EPS) * scale))
        return jnp.concatenate(outs, axis=-1) if len(outs) > 1 else outs[0]

    @pl.when(j < nq)
    def _():
        p_ref[...] = l2n(conv_silu(acc), hdim ** -0.5).astype(BF16)

    @pl.when((j >= nq) & (j < nq + nz))
    def _():
        p_ref[...] = acc.astype(BF16)

    @pl.when((j >= nq + nz) & (j < nq + nz + nk))
    def _():
        p_ref[...] = l2n(conv_silu(acc), 1.0).astype(BF16)

    @pl.when(j >= nq + nz + nk)
    def _():
        p_ref[...] = conv_silu(acc).astype(BF16)


def _gdn_in(x, nw, mod, w_main, w_gate, cw, gp, tile, tiles_per_mod, period, j0, nj, tn, sect, hv, hdim):
    n, d = x.shape
    nq, nz, nk = sect
    kern = functools.partial(_gdn_in_kernel, period=period, j0=j0, nq=nq, nz=nz, nk=nk, hv=hv, hdim=hdim)
    return pl.pallas_call(
        kern,
        grid=(n // tile, nj),
        in_specs=[pl.BlockSpec((tile, d), lambda i, j: (i, 0)),
                  pl.BlockSpec((1, d), lambda i, j: (0, 0)),
                  pl.BlockSpec((1, 6, d), lambda i, j: (i // tiles_per_mod, 0, 0)),
                  pl.BlockSpec((d, tn), lambda i, j: (0, j + j0)),
                  pl.BlockSpec((d, LANES), lambda i, j: (0, 0)),
                  pl.BlockSpec((8, tn), lambda i, j: (0, j + j0)),
                  pl.BlockSpec((2, LANES), lambda i, j: (0, 0))],
        out_specs=[pl.BlockSpec((tile, tn), lambda i, j: (i, j)),
                   pl.BlockSpec((tile, LANES), lambda i, j: (i, 0)),
                   pl.BlockSpec((tile // GDN_CHUNK, LANES, GDN_CHUNK), lambda i, j: (i, 0, 0))],
        out_shape=[jax.ShapeDtypeStruct((n, nj * tn), BF16),
                   jax.ShapeDtypeStruct((n, LANES), F32),
                   jax.ShapeDtypeStruct((n // GDN_CHUNK, LANES, GDN_CHUNK), F32)],
        scratch_shapes=[pltpu.VMEM((tile, d), BF16)],
        compiler_params=_params("parallel", "arbitrary"),
        name="gdn_in",
    )(x, nw.reshape(1, d), mod, w_main, w_gate, cw, gp)


def _substitution_masks(n):
    r, c = np.meshgrid(np.arange(n), np.arange(n), indexing="ij")
    out = []
    for lower in (True, False):
        lv = []
        b = 1
        while b < n:
            same = (r // (2 * b)) == (c // (2 * b))
            hi_r, hi_c = (r % (2 * b)) >= b, (c % (2 * b)) >= b
            lv.append(same & ((hi_r & ~hi_c) if lower else (~hi_r & hi_c)))
            b *= 2
        out.append(np.stack(lv))
    return np.stack(out).astype(np.float32)


def _gdn_kernel(q_ref, k_ref, v_ref, z_ref, kc_ref, vc_ref, g_ref, gt_ref, gc_ref, gct_ref, ow_ref, lvl_ref,
                o_ref, oacc, s_scr, *, hv, rep, hdim, ncc, ncl):
    ch = GDN_CHUNK
    head0 = pl.program_id(1) * rep
    lane = lax.broadcasted_iota(jnp.int32, (1, LANES), 1)
    ri = lax.broadcasted_iota(jnp.int32, (ch, ch), 0)
    ci = lax.broadcasted_iota(jnp.int32, (ch, ch), 1)
    eye = jnp.where(ri == ci, 1.0, 0.0)
    masks = ((ci <= ri, ci < ri), (ci >= ri, ci > ri))
    s_scr[...] = jnp.zeros(s_scr.shape, F32)
    oacc[...] = jnp.zeros(oacc.shape, F32)

    def column(gt, idx):
        return jnp.sum(jnp.where(lane == idx, gt, 0.0), axis=1, keepdims=True)

    def chunk(kq_refs, v_r, g_r, gt_r, c, dirn, with_out):
        r0 = pl.multiple_of(c * ch, ch)
        rows = pl.ds(r0, ch)
        k_r, q_r = kq_refs
        kb = k_r[rows, :]
        kf = kb.astype(F32)
        kk = _dot_nt(kb, kb)
        if with_out:
            qb = q_r[rows, :]
            qf = qb.astype(F32)
            qk = _dot_nt(qb, kb)
        gtile = g_r[rows, :]
        incl, strict = masks[dirn]
        for hl in range(rep):
            head = head0 + hl
            beta = column(gtile, dirn * hv + head)
            gidx = (2 + dirn) * hv + head
            gcol = column(gtile, gidx)
            grow = gt_r[c, pl.ds(gidx, 1), :]
            decay = jnp.exp(jnp.where(incl, gcol - grow, -jnp.inf))
            a = jnp.where(strict, beta * kk * decay, 0.0)
            tm = eye - a * lvl_ref[dirn, 0]
            for lvl in range(1, lvl_ref.shape[1]):
                tb = tm.astype(BF16)
                tm = tm - _dot(tb, _dot((a * lvl_ref[dirn, lvl]).astype(BF16), tb).astype(BF16))
            eg = jnp.exp(gcol)
            vf = v_r[rows, hl * hdim:(hl + 1) * hdim].astype(F32)
            rhs = jnp.concatenate([vf * beta, kf * (beta * eg)], axis=1).astype(BF16)
            uw = _dot(tm.astype(BF16), rhs)
            u, w = uw[:, :hdim], uw[:, hdim:]
            glast = grow[:, ch - 1:ch] if dirn == 0 else grow[:, 0:1]
            kend = (kf * jnp.exp(glast - gcol)).astype(BF16)
            sidx = dirn * rep + hl
            s = s_scr[sidx]
            sb = s.astype(BF16)
            if with_out:
                ws = _dot(jnp.concatenate([w, qf * eg], axis=0).astype(BF16), sb)
                vnew = (u - ws[:ch]).astype(BF16)
                intra = jnp.where(incl, qk * decay, 0.0).astype(BF16)
                o = ws[ch:] + _dot(intra, vnew)
                oacc[rows, hl * hdim:(hl + 1) * hdim] += o
            else:
                vnew = (u - _dot(w.astype(BF16), sb)).astype(BF16)
            s_scr[sidx] = s * jnp.exp(glast) + _dot_tn(kend, vnew)

    def ctx_body(i, carry):
        chunk((kc_ref, None), vc_ref, gc_ref, gct_ref, i, 0, False)
        chunk((kc_ref, None), vc_ref, gc_ref, gct_ref, ncc - 1 - i, 1, False)
        return carry

    def lat_body(i, carry):
        chunk((k_ref, q_ref), v_ref, g_ref, gt_ref, i, 0, True)
        chunk((k_ref, q_ref), v_ref, g_ref, gt_ref, ncl - 1 - i, 1, True)
        return carry

    lax.fori_loop(0, ncc, ctx_body, 0)
    lax.fori_loop(0, ncl, lat_body, 0)

    blk = min(256, ncl * ch)
    ow = ow_ref[...]
    for r0 in range(0, ncl * ch, blk):
        for hl in range(rep):
            cs = slice(hl * hdim, (hl + 1) * hdim)
            o = oacc[r0:r0 + blk, cs]
            o = o * lax.rsqrt(jnp.mean(o * o, axis=-1, keepdims=True) + NORM_EPS) * ow
            o_ref[r0:r0 + blk, cs] = (o * _silu(z_ref[r0:r0 + blk, cs].astype(F32))).astype(BF16)


def _gdn_core(p_lat, p_ctx, g_lat, gt_lat, g_ctx, gt_ctx, onorm_w, bsz, hk, hv, hdim):
    l = p_lat.shape[0] // bsz
    lc = p_ctx.shape[0] // bsz
    rep = hv // hk
    vw = rep * hdim
    qk, vd = hk * hdim, hv * hdim
    kern = functools.partial(_gdn_kernel, hv=hv, rep=rep, hdim=hdim, ncc=lc // GDN_CHUNK, ncl=l // GDN_CHUNK)
    kcol, vcol, zcol = (qk + vd) // hdim, (2 * qk + vd) // vw, qk // vw
    lvl = jnp.asarray(_substitution_masks(GDN_CHUNK))
    return pl.pallas_call(
        kern,
        grid=(bsz, hk),
        in_specs=[pl.BlockSpec((l, hdim), lambda b, h: (b, h)),
                  pl.BlockSpec((l, hdim), lambda b, h: (b, kcol + h)),
                  pl.BlockSpec((l, vw), lambda b, h: (b, vcol + h)),
                  pl.BlockSpec((l, vw), lambda b, h: (b, zcol + h)),
                  pl.BlockSpec((lc, hdim), lambda b, h: (b, h)),
                  pl.BlockSpec((lc, vw), lambda b, h: (b, qk // vw + h)),
                  pl.BlockSpec((l, LANES), lambda b, h: (b, 0)),
                  pl.BlockSpec((l // GDN_CHUNK, LANES, GDN_CHUNK), lambda b, h: (b, 0, 0)),
                  pl.BlockSpec((lc, LANES), lambda b, h: (b, 0)),
                  pl.BlockSpec((lc // GDN_CHUNK, LANES, GDN_CHUNK), lambda b, h: (b, 0, 0)),
                  pl.BlockSpec((1, hdim), lambda b, h: (0, 0)),
                  pl.BlockSpec(lvl.shape, lambda b, h: (0, 0, 0, 0))],
        out_specs=pl.BlockSpec((l, vw), lambda b, h: (b, h)),
        out_shape=jax.ShapeDtypeStruct((bsz * l, vd), BF16),
        scratch_shapes=[pltpu.VMEM((l, vw), F32), pltpu.VMEM((2 * rep, hdim, hdim), F32)],
        compiler_params=_params("parallel", "arbitrary"),
        name="gdn_core",
    )(p_lat, p_lat, p_lat, p_lat, p_ctx, p_ctx, g_lat, gt_lat, g_ctx, gt_ctx, onorm_w.reshape(1, hdim), lvl)


def kernel(x, c, ctx, c_ctx, ada_w, ada_b, norm1_w, norm2_w, mlp_w1, mlp_w2, s5_lam_re, s5_lam_im, s5_log_dt, s5_b_re, s5_b_im, s5_c_re, s5_c_im, s5_d, s5_w_glu, gdn_w_in, gdn_conv_w, gdn_a_log, gdn_dt_bias, gdn_onorm_w, gdn_w_out, final_norm_w):
    bsz, l, d = x.shape
    lc = ctx.shape[1]
    depth = ada_w.shape[0]
    assert depth == 2 and s5_lam_re.shape[0] == 1 and gdn_w_in.shape[0] == 1
    groups = s5_lam_re.shape[2]
    hv = gdn_a_log.shape[2]
    hdim = gdn_onorm_w.shape[1]
    vd = gdn_w_out.shape[1]
    qk = (gdn_conv_w.shape[2] - vd) // 2
    hk = qk // hdim
    assert l % GRID_W == 0 and l % GDN_CHUNK == 0 and lc % GDN_CHUNK == 0
    assert l % S5_CHUNK == 0 and lc % S5_CHUNK == 0 and 4 * hv <= LANES

    tb_l, tb_c = min(256, l), lc
    tm_l, tm_c = min(512, l), min(512, bsz * lc)
    xl = x.reshape(bsz * l, d)
    xc = ctx.reshape(bsz * lc, d)

    rows = -(-(bsz + 1) // 8) * 8
    cvec = jnp.zeros((rows, d), F32).at[:bsz].set(c).at[bsz].set(c_ctx)
    mod = _ada_mod(cvec, ada_w, ada_b).reshape(depth, rows, 6, d)
    big = 1 << 30

    mod_l, mod_c = mod[0, :bsz], mod[0, bsz:bsz + 1]
    hl = _prenorm(xl, norm1_w[0], mod_l, tb_l, l // tb_l)
    hc = _prenorm(xc, norm1_w[0], mod_c, tb_c, big)
    tables = _s5_tables(s5_lam_re[0], s5_lam_im[0], s5_log_dt[0], s5_b_re[0], s5_b_im[0],
                        s5_c_re[0], s5_c_im[0], s5_d[0], S5_CHUNK)
    yc, yl = _s5_core(_to_groups(hc, bsz, groups, S5_CHUNK), _to_groups(hl, bsz, groups, S5_CHUNK), tables, bsz)
    w_glu = s5_w_glu[0].astype(BF16)
    w1, w2 = mlp_w1[0].astype(BF16), mlp_w2[0].astype(BF16)
    xl = _mixout(_from_groups(yl, bsz, S5_CHUNK), w_glu, xl, mod_l, tm_l, l // tm_l, True)
    xc = _mixout(_from_groups(yc, bsz, S5_CHUNK), w_glu, xc, mod_c, tm_c, big, True)
    xl = _mlp(xl, norm2_w[0], mod_l, w1, w2, final_norm_w, tm_l, l // tm_l, False)
    xc = _mlp(xc, norm2_w[0], mod_c, w1, w2, final_norm_w, tm_c, big, False)

    mod_l, mod_c = mod[1, :bsz], mod[1, bsz:bsz + 1]
    w_in = gdn_w_in[0]
    n_main = 2 * qk + 2 * vd
    w_main = w_in[:, :n_main].astype(BF16)
    w_gate = jnp.pad(w_in[:, n_main:], ((0, 0), (0, LANES - 4 * hv)))
    conv_w = gdn_conv_w[0].astype(F32)
    cw = jnp.concatenate([conv_w[:, :qk], jnp.zeros((conv_w.shape[0], vd), F32), conv_w[:, qk:]], axis=1)
    cw = jnp.pad(cw, ((0, 8 - cw.shape[0]), (0, 0)))
    neg_a = jnp.pad(-jnp.exp(gdn_a_log[0].astype(F32)).reshape(-1), (2 * hv, LANES - 4 * hv))
    dtb = jnp.pad(gdn_dt_bias[0].astype(F32).reshape(-1), (2 * hv, LANES - 4 * hv))
    gp = jnp.stack([neg_a, dtb])
    tn = min(512, qk)
    sect = (qk // tn, vd // tn, qk // tn)
    nj = n_main // tn
    jkv = (qk + vd) // tn
    p_l, g_l, gt_l = _gdn_in(xl, norm1_w[1], mod_l, w_main, w_gate, cw, gp, tb_l, l // tb_l, GRID_W,
                             0, nj, tn, sect, hv, hdim)
    p_c, g_c, gt_c = _gdn_in(xc, norm1_w[1], mod_c, w_main, w_gate, cw, gp, tb_c, big, lc,
                             jkv, nj - jkv, tn, sect, hv, hdim)
    gated = _gdn_core(p_l, p_c, g_l, gt_l, g_c, gt_c, gdn_onorm_w[0].astype(F32), bsz, hk, hv, hdim)
    xl = _mixout(gated, gdn_w_out[0].astype(BF16), xl, mod_l, tm_l, l // tm_l, False)
    out = _mlp(xl, norm2_w[1], mod_l, mlp_w1[1].astype(BF16), mlp_w2[1].astype(BF16), final_norm_w,
               tm_l, l // tm_l, True)
    return out.reshape(bsz, l, d)
```

```python
import functools
import math

import jax
import jax.numpy as jnp
import numpy as np
from jax import lax
from jax.experimental import pallas as pl
from jax.experimental.pallas import tpu as pltpu

F32 = jnp.float32
BF16 = jnp.bfloat16
HIGHEST = lax.Precision.HIGHEST

NORM_EPS = 1e-6
GRID_W = 64
GDN_CHUNK = 64
GDN_KEY_HEADS_PER_STEP = 2
GDN_PREP_UNROLL = 4
S5_CHUNK = 16
LANES = 128
VMEM_LIMIT = 56 * 1024 * 1024


def _dot(a, b, precision=None):
    return jnp.dot(a, b, preferred_element_type=F32, precision=precision)


def _dot_nt(a, b):
    return lax.dot_general(a, b, (((1,), (1,)), ((), ())), preferred_element_type=F32)


def _dot_tn(a, b):
    return lax.dot_general(a, b, (((0,), (0,)), ((), ())), preferred_element_type=F32)


def _floor_pow2(v, n):
    assert n & (n - 1) == 0
    return jnp.bitwise_and(v, -n)


def _silu(t):
    return t * jax.nn.sigmoid(t)


def _norm_mod(x, nw, shift, scale):
    y = x * lax.rsqrt(jnp.mean(x * x, axis=-1, keepdims=True) + NORM_EPS) * nw
    return y * (1.0 + scale) + shift


def _params(*sem):
    return pltpu.CompilerParams(dimension_semantics=sem, vmem_limit_bytes=VMEM_LIMIT)


def _ada_kernel(c_ref, w_ref, b_ref, o_ref):
    o_ref[0] = _dot(_silu(c_ref[...]), w_ref[0], HIGHEST) + b_ref[0]


def _ada_mod(cvec, ada_w, ada_b):
    depth, d, n = ada_w.shape
    rows = cvec.shape[0]
    tn = n // 4
    return pl.pallas_call(
        _ada_kernel,
        grid=(depth, n // tn),
        in_specs=[pl.BlockSpec((rows, d), lambda i, j: (0, 0)),
                  pl.BlockSpec((1, d, tn), lambda i, j: (i, 0, j)),
                  pl.BlockSpec((1, 1, tn), lambda i, j: (i, 0, j))],
        out_specs=pl.BlockSpec((1, rows, tn), lambda i, j: (i, 0, j)),
        out_shape=jax.ShapeDtypeStruct((depth, rows, n), F32),
        compiler_params=_params("arbitrary", "arbitrary"),
        name="ada_mod",
    )(cvec, ada_w, ada_b.reshape(depth, 1, n))


def _prenorm_kernel(x_ref, nw_ref, mod_ref, o_ref):
    h = _norm_mod(x_ref[...], nw_ref[...], mod_ref[0, 0:1, :], mod_ref[0, 1:2, :])
    o_ref[...] = h.astype(o_ref.dtype)


def _prenorm(x, nw, mod, tile, tiles_per_mod):
    n, d = x.shape
    return pl.pallas_call(
        _prenorm_kernel,
        grid=(n // tile,),
        in_specs=[pl.BlockSpec((tile, d), lambda i: (i, 0)),
                  pl.BlockSpec((1, d), lambda i: (0, 0)),
                  pl.BlockSpec((1, 6, d), lambda i: (i // tiles_per_mod, 0, 0))],
        out_specs=pl.BlockSpec((tile, d), lambda i: (i, 0)),
        out_shape=jax.ShapeDtypeStruct((n, d), BF16),
        compiler_params=_params("parallel"),
        name="prenorm",
    )(x, nw.reshape(1, d), mod)


def _s5_tables(lam_re, lam_im, log_dt, b_re, b_im, c_re, c_im, d_skip, t):
    _, g, p = lam_re.shape
    h = b_re.shape[-1]
    f = lambda a: a.astype(F32)
    lam_re, lam_im, b_re, b_im, c_re, c_im = map(f, (lam_re, lam_im, b_re, b_im, c_re, c_im))
    dt = jnp.exp(f(log_dt))[..., None]
    zr, zi = lam_re * dt, lam_im * dt
    n = jnp.arange(t + 1, dtype=F32)[:, None, None, None]
    mag = jnp.exp(n * zr)
    pr, pi = mag * jnp.cos(n * zi), mag * jnp.sin(n * zi)
    ar, ai = pr[1], pi[1]
    den = lam_re * lam_re + lam_im * lam_im
    fr = ((ar - 1.0) * lam_re + ai * lam_im) / den
    fi = (ai * lam_re - (ar - 1.0) * lam_im) / den
    bbr = fr[..., None] * b_re - fi[..., None] * b_im
    bbi = fr[..., None] * b_im + fi[..., None] * b_re
    car = c_re[None] * pr[:, :, :, None, :] - c_im[None] * pi[:, :, :, None, :]
    cai = c_re[None] * pi[:, :, :, None, :] + c_im[None] * pr[:, :, :, None, :]
    kern = (jnp.einsum('ndgop,dgpi->ndgoi', car, bbr, precision=HIGHEST)
            - jnp.einsum('ndgop,dgpi->ndgoi', cai, bbi, precision=HIGHEST))
    ti = jnp.arange(t)
    lag = ti[None, :] - ti[:, None]
    kf = jnp.where((lag >= 0)[:, :, None, None, None], kern[jnp.clip(lag, 0, t), 0], 0.0)
    kb = jnp.where((lag <= 0)[:, :, None, None, None], kern[jnp.clip(-lag, 0, t), 1], 0.0)
    intra = jnp.transpose(kf + kb, (2, 0, 4, 1, 3)).reshape(g, t * h, t * h)

    def inject(d, powers):
        er, ei = pr[powers, d], pi[powers, d]
        re = er[..., None] * bbr[d][None] - ei[..., None] * bbi[d][None]
        im = er[..., None] * bbi[d][None] + ei[..., None] * bbr[d][None]
        tr = lambda a: jnp.transpose(a, (1, 0, 3, 2)).reshape(g, t * h, p)
        return tr(re), tr(im)

    pad = lambda a: jnp.pad(a, ((0, 0), (0, 0), (0, LANES - p)))
    in_f = inject(0, t - 1 - ti)
    in_b = inject(1, ti)
    w1 = jnp.concatenate([intra] + [pad(a) for a in (*in_f, *in_b)], axis=-1)

    def readout(d, powers):
        tr = lambda a: jnp.transpose(a, (1, 3, 0, 2)).reshape(g, p, t * h)
        return tr(car[powers, d]), tr(-cai[powers, d])

    padr = lambda a: jnp.pad(a, ((0, 0), (0, LANES - p), (0, 0)))
    w2 = jnp.concatenate([padr(a) for a in (*readout(0, ti + 1), *readout(1, t - ti))], axis=1)
    dec = jnp.stack([pr[t, 0], pi[t, 0], pr[t, 1], pi[t, 1]], axis=1)
    dec = jnp.pad(dec, ((0, 0), (0, 4), (0, LANES - p)))
    dsk = jnp.tile(f(d_skip).reshape(g, 1, h), (1, t, 1)).reshape(g, 1, t * h)
    return w1.astype(BF16), w2.astype(BF16), dec, dsk


def _s5_kernel(uc_ref, ul_ref, w1_ref, w2_ref, dec_ref, dsk_ref, yc_ref, yl_ref, z_scr, h_scr,
               *, bsz, ncc, ncl, width):
    rc, rl = ncc * bsz, ncl * bsz
    w1 = w1_ref[0]
    z_scr[0:rc, :] = _dot(uc_ref[0], w1)
    z_scr[rc:rc + rl, :] = _dot(ul_ref[0], w1)
    dec = dec_ref[0]
    zero = jnp.zeros((bsz, LANES), F32)
    cols = [width + k * LANES for k in range(5)]

    def make_step(are, aim, col, hcol):
        def step(row, carry):
            re, im = carry
            r = pl.multiple_of(row, bsz)
            h_scr[pl.ds(r, bsz), hcol:hcol + LANES] = re
            h_scr[pl.ds(r, bsz), hcol + LANES:hcol + 2 * LANES] = im
            s_re = z_scr[pl.ds(r, bsz), cols[col]:cols[col + 1]]
            s_im = z_scr[pl.ds(r, bsz), cols[col + 1]:cols[col + 2]]
            return are * re - aim * im + s_re, are * im + aim * re + s_im
        return step

    fstep = make_step(dec[0:1, :], dec[1:2, :], 0, 0)
    bstep = make_step(dec[2:3, :], dec[3:4, :], 2, 2 * LANES)
    lax.fori_loop(0, ncc + ncl, lambda c, s: fstep(c * bsz, s), (zero, zero))
    st = lax.fori_loop(0, ncc, lambda i, s: bstep((ncc - 1 - i) * bsz, s), (zero, zero))
    lax.fori_loop(0, ncl, lambda i, s: bstep((ncc + ncl - 1 - i) * bsz, s), st)

    w2 = w2_ref[0]
    dsk = dsk_ref[0]
    blk = min(256, rc)
    for u_ref, y_ref, base, rows in ((uc_ref, yc_ref, 0, rc), (ul_ref, yl_ref, rc, rl)):
        for r0 in range(0, rows, blk):
            sl = slice(base + r0, base + r0 + blk)
            y = z_scr[sl, 0:width] + _dot(h_scr[sl, :].astype(BF16), w2)
            y = y + dsk * u_ref[0, r0:r0 + blk, :].astype(F32)
            y_ref[0, r0:r0 + blk, :] = jax.nn.gelu(y).astype(BF16)


def _s5_core(uc, ul, tables, bsz):
    w1, w2, dec, dsk = tables
    g, rc, width = uc.shape
    rl = ul.shape[1]
    kern = functools.partial(_s5_kernel, bsz=bsz, ncc=rc // bsz, ncl=rl // bsz, width=width)
    gmap = lambda i: (i, 0, 0)
    return pl.pallas_call(
        kern,
        grid=(g,),
        in_specs=[pl.BlockSpec((1, rc, width), gmap), pl.BlockSpec((1, rl, width), gmap),
                  pl.BlockSpec((1,) + w1.shape[1:], gmap), pl.BlockSpec((1,) + w2.shape[1:], gmap),
                  pl.BlockSpec((1,) + dec.shape[1:], gmap), pl.BlockSpec((1,) + dsk.shape[1:], gmap)],
        out_specs=[pl.BlockSpec((1, rc, width), gmap), pl.BlockSpec((1, rl, width), gmap)],
        out_shape=[jax.ShapeDtypeStruct(uc.shape, BF16), jax.ShapeDtypeStruct(ul.shape, BF16)],
        scratch_shapes=[pltpu.VMEM((rc + rl, w1.shape[2]), F32), pltpu.VMEM((rc + rl, 4 * LANES), F32)],
        compiler_params=_params("parallel"),
        name="s5_core",
    )(uc, ul, w1, w2, dec, dsk)


def _to_groups(hm, bsz, groups, t):
    n, d = hm.shape
    l = n // bsz
    hch = d // groups
    a = hm.reshape(bsz, l // t, t, groups, hch)
    return jnp.transpose(a, (3, 1, 0, 2, 4)).reshape(groups, (l // t) * bsz, t * hch)


def _from_groups(y, bsz, t):
    g, rows, width = y.shape
    nc = rows // bsz
    hch = width // t
    a = y.reshape(g, nc, bsz, t, hch)
    return jnp.transpose(a, (2, 1, 3, 0, 4)).reshape(bsz * nc * t, g * hch)


def _mixout_kernel(y_ref, w_ref, x_ref, mod_ref, o_ref, *, glu):
    r = _dot(y_ref[...], w_ref[...])
    if glu:
        half = r.shape[-1] // 2
        r = r[:, :half] * jax.nn.sigmoid(r[:, half:])
    o_ref[...] = x_ref[...] + mod_ref[0, 2:3, :] * r


def _mixout(y, w, x, mod, tile, tiles_per_mod, glu):
    n, d = x.shape
    k, nn = w.shape
    return pl.pallas_call(
        functools.partial(_mixout_kernel, glu=glu),
        grid=(n // tile,),
        in_specs=[pl.BlockSpec((tile, k), lambda i: (i, 0)),
                  pl.BlockSpec((k, nn), lambda i: (0, 0)),
                  pl.BlockSpec((tile, d), lambda i: (i, 0)),
                  pl.BlockSpec((1, 6, d), lambda i: (i // tiles_per_mod, 0, 0))],
        out_specs=pl.BlockSpec((tile, d), lambda i: (i, 0)),
        out_shape=jax.ShapeDtypeStruct((n, d), F32),
        compiler_params=_params("parallel"),
        name="mixer_out",
    )(y, w, x, mod)


def _mlp_kernel(x_ref, nw_ref, mod_ref, w1_ref, w2_ref, fw_ref, o_ref, *, ff_blk, final):
    x = x_ref[...]
    h = _norm_mod(x, nw_ref[...], mod_ref[0, 3:4, :], mod_ref[0, 4:5, :]).astype(BF16)
    acc = jnp.zeros(x.shape, F32)
    for f0 in range(0, w1_ref.shape[1], ff_blk):
        a = jnp.maximum(_dot(h, w1_ref[:, f0:f0 + ff_blk]), 0.0)
        acc = acc + _dot((a * a).astype(BF16), w2_ref[f0:f0 + ff_blk, :])
    y = x + mod_ref[0, 5:6, :] * acc
    if final:
        y = y * lax.rsqrt(jnp.mean(y * y, axis=-1, keepdims=True) + NORM_EPS) * fw_ref[...]
    o_ref[...] = y


def _mlp(x, nw, mod, w1, w2, fw, tile, tiles_per_mod, final):
    n, d = x.shape
    ff = w1.shape[1]
    return pl.pallas_call(
        functools.partial(_mlp_kernel, ff_blk=min(1024, ff), final=final),
        grid=(n // tile,),
        in_specs=[pl.BlockSpec((tile, d), lambda i: (i, 0)),
                  pl.BlockSpec((1, d), lambda i: (0, 0)),
                  pl.BlockSpec((1, 6, d), lambda i: (i // tiles_per_mod, 0, 0)),
                  pl.BlockSpec((d, ff), lambda i: (0, 0)),
                  pl.BlockSpec((ff, d), lambda i: (0, 0)),
                  pl.BlockSpec((1, d), lambda i: (0, 0))],
        out_specs=pl.BlockSpec((tile, d), lambda i: (i, 0)),
        out_shape=jax.ShapeDtypeStruct((n, d), F32),
        compiler_params=_params("parallel"),
        name="mlp",
    )(x, nw.reshape(1, d), mod, w1, w2, fw.reshape(1, d))


def _gdn_in_kernel(x_ref, nw_ref, mod_ref, w_ref, wg_ref, cw_ref, gp_ref, p_ref, g_ref, gt_ref,
                   *, period, col0, qk, vd, tn, hv, hdim):
    tb = x_ref.shape[0]
    h = _norm_mod(x_ref[...], nw_ref[...], mod_ref[0, 0:1, :], mod_ref[0, 1:2, :])
    hb = h.astype(BF16)
    gl = _dot(h, wg_ref[...], HIGHEST)
    lane = lax.broadcasted_iota(jnp.int32, (1, LANES), 1)
    beta = jax.nn.sigmoid(gl)
    t = gl + gp_ref[1:2, :]
    g = gp_ref[0:1, :] * (jnp.maximum(t, 0.0) + jnp.log1p(jnp.exp(-jnp.abs(t))))
    r = lax.broadcasted_iota(jnp.int32, (tb, tb), 0)
    c = lax.broadcasted_iota(jnp.int32, (tb, tb), 1)
    same = _floor_pow2(r, GDN_CHUNK) == _floor_pow2(c, GDN_CHUNK)
    gcf = _dot(jnp.where(same & (c <= r), 1.0, 0.0), g, HIGHEST)
    gcb = _dot(jnp.where(same & (c >= r), 1.0, 0.0), g, HIGHEST)
    out = jnp.where(lane < 2 * hv, beta, jnp.where(lane < 3 * hv, gcf, jnp.where(lane < 4 * hv, gcb, 0.0)))
    g_ref[...] = out
    for cc in range(tb // GDN_CHUNK):
        gt_ref[cc] = out[cc * GDN_CHUNK:(cc + 1) * GDN_CHUNK, :].T

    row = lax.broadcasted_iota(jnp.int32, (tb, 1), 0)
    pos = row - _floor_pow2(row, period)
    taps = [(tap, ((pos + tap - 2 >= 0) & (pos + tap - 2 < period)).astype(F32)) for tap in (0, 1, 3, 4)]

    def conv_silu(a, cw):
        y = a * cw[2:3, :]
        for tap, ok in taps:
            sh = pltpu.roll(a, (tb - (tap - 2)) % tb, 0)
            y = y + sh * (ok * cw[tap:tap + 1, :])
        return _silu(y)

    def l2n(y, scale):
        outs = []
        for h0 in range(0, tn, hdim):
            yh = y[:, h0:h0 + hdim]
            outs.append(yh * (lax.rsqrt(jnp.sum(yh * yh, axis=-1, keepdims=True) + NORM_EPS) * scale))
        return jnp.concatenate(outs, axis=-1) if len(outs) > 1 else outs[0]

    for o0 in range(0, p_ref.shape[1], tn):
        c0 = col0 + o0
        acc = _dot(hb, w_ref[:, c0:c0 + tn])
        if c0 < qk:
            res = l2n(conv_silu(acc, cw_ref[:, c0:c0 + tn]), hdim ** -0.5)
        elif c0 < qk + vd:
            res = acc
        elif c0 < 2 * qk + vd:
            res = l2n(conv_silu(acc, cw_ref[:, c0:c0 + tn]), 1.0)
        else:
            res = conv_silu(acc, cw_ref[:, c0:c0 + tn])
        p_ref[:, o0:o0 + tn] = res.astype(BF16)


def _gdn_in(x, nw, mod, w_main, w_gate, cw, gp, tile, tiles_per_mod, period, col0, tn, qk, vd, hv, hdim):
    n, d = x.shape
    n_main = w_main.shape[1]
    kern = functools.partial(_gdn_in_kernel, period=period, col0=col0, qk=qk, vd=vd, tn=tn, hv=hv, hdim=hdim)
    return pl.pallas_call(
        kern,
        grid=(n // tile,),
        in_specs=[pl.BlockSpec((tile, d), lambda i: (i, 0)),
                  pl.BlockSpec((1, d), lambda i: (0, 0)),
                  pl.BlockSpec((1, 6, d), lambda i: (i // tiles_per_mod, 0, 0)),
                  pl.BlockSpec((d, n_main), lambda i: (0, 0)),
                  pl.BlockSpec((d, LANES), lambda i: (0, 0)),
                  pl.BlockSpec((8, n_main), lambda i: (0, 0)),
                  pl.BlockSpec((2, LANES), lambda i: (0, 0))],
        out_specs=[pl.BlockSpec((tile, n_main - col0), lambda i: (i, 0)),
                   pl.BlockSpec((tile, LANES), lambda i: (i, 0)),
                   pl.BlockSpec((tile // GDN_CHUNK, LANES, GDN_CHUNK), lambda i: (i, 0, 0))],
        out_shape=[jax.ShapeDtypeStruct((n, n_main - col0), BF16),
                   jax.ShapeDtypeStruct((n, LANES), F32),
                   jax.ShapeDtypeStruct((n // GDN_CHUNK, LANES, GDN_CHUNK), F32)],
        compiler_params=_params("parallel"),
        name="gdn_in",
    )(x, nw.reshape(1, d), mod, w_main, w_gate, cw, gp)


def _gdn_masks(ch, rep):
    nco = 2 * rep
    r, c = np.meshgrid(np.arange(ch), np.arange(ch), indexing="ij")
    per_dir = []
    for lower in (True, False):
        m = [(c <= r) if lower else (c >= r), (c < r) if lower else (c > r)]
        b = 1
        while b < ch:
            same = (r // (2 * b)) == (c // (2 * b))
            hi_r, hi_c = (r % (2 * b)) >= b, (c % (2 * b)) >= b
            m.append(same & ((hi_r & ~hi_c) if lower else (~hi_r & hi_c)))
            b *= 2
        per_dir.append(np.stack(m))
    cm = np.concatenate([per_dir[j // rep] for j in range(nco)], axis=2).astype(np.float32)
    blk = np.arange(nco * ch) // ch
    return cm, (blk[:, None] == blk[None, :]).astype(np.float32)


def _gdn_kernel(q_ref, k_ref, v_ref, z_ref, kc_ref, vc_ref, g_ref, gt_ref, gc_ref, gct_ref, ow_ref, cm_ref, bd_ref,
                o_ref, of_s, ob_s, st_s, uw_s, in_s, kt_s, *, hv, khb, rep, hdim, ncc, ncl, unroll):
    ch = GDN_CHUNK
    nco = 2 * rep
    nlvl = cm_ref.shape[0] - 2
    kh0 = pl.program_id(1) * khb
    lane = lax.broadcasted_iota(jnp.int32, (1, LANES), 1)
    lcat = lax.broadcasted_iota(jnp.int32, (1, nco * ch), 1)

    def lanes_of(khl, j):
        head = (kh0 + khl) * rep + j % rep
        return (j // rep) * hv + head, (2 + j // rep) * hv + head

    def column(gt, idx):
        return jnp.sum(jnp.where(lane == idx, gt, 0.0), axis=1, keepdims=True)

    def spread(cols):
        out = jnp.broadcast_to(cols[nco - 1], (ch, nco * ch))
        for j in range(nco - 2, -1, -1):
            out = jnp.where(lcat < (j + 1) * ch, cols[j], out)
        return out

    def block_diag(xb):
        return jnp.concatenate([xb] * nco, axis=0) * bd_ref[...]

    def prepare(insts):
        st = []
        for khl, k_r, q_r, v_r, g_r, gt_r, c, slot in insts:
            rows = pl.ds(pl.multiple_of(c * ch, ch), ch)
            kb = k_r[rows, khl * hdim:(khl + 1) * hdim]
            kcat = jnp.concatenate([kb] * nco, axis=0)
            if q_r is None:
                kkc, qkc = _dot_nt(kb, kcat), None
            else:
                prod = _dot_nt(jnp.concatenate([kb, q_r[rows, khl * hdim:(khl + 1) * hdim]], axis=0), kcat)
                kkc, qkc = prod[:ch], prod[ch:]
            gtile = g_r[rows, :]
            betas, gcols, grows = [], [], []
            for j in range(nco):
                bl, gl = lanes_of(khl, j)
                betas.append(column(gtile, bl))
                gcols.append(column(gtile, gl))
                grows.append(gt_r[c, pl.ds(gl, 1), :])
            dlog = spread(gcols) - jnp.concatenate(grows, axis=1)
            decay = jnp.exp(jnp.where(cm_ref[0] > 0.0, dlog, -jnp.inf))
            ac = spread(betas) * kkc * decay * cm_ref[1]
            xc = (cm_ref[0] - cm_ref[1]) - ac * cm_ref[2]
            st.append([rows, kb, qkc, betas, gcols, decay, ac, xc])
        for lvl in range(1, nlvl):
            xbs = [s[7].astype(BF16) for s in st]
            ps = [_dot((s[6] * cm_ref[2 + lvl]).astype(BF16), block_diag(xb)) for s, xb in zip(st, xbs)]
            for s, xb, p in zip(st, xbs, ps):
                s[7] = s[7] - _dot(xb, block_diag(p.astype(BF16)))
        outs = []
        for (khl, k_r, q_r, v_r, g_r, gt_r, c, slot), (rows, kb, qkc, betas, gcols, decay, ac, xc) in zip(insts, st):
            kf = kb.astype(F32)
            rst = []
            for j in range(nco):
                v0 = (khl * rep + j % rep) * hdim
                vf = v_r[rows, v0:v0 + hdim].astype(F32)
                rst.append(jnp.concatenate([vf * betas[j], kf * (betas[j] * jnp.exp(gcols[j]))], axis=1))
            uw = _dot(block_diag(xc.astype(BF16)), jnp.concatenate(rst, axis=0).astype(BF16))
            intra = None if qkc is None else (qkc * decay * cm_ref[0]).astype(BF16)
            outs.append((uw, kf.T.astype(BF16), intra))
        for (khl, k_r, q_r, v_r, g_r, gt_r, c, slot), (uw, kt, intra) in zip(insts, outs):
            for j in range(nco):
                uw_s[slot, khl * nco + j] = uw[j * ch:(j + 1) * ch, :].astype(BF16)
                if intra is not None:
                    in_s[c, khl * nco + j] = intra[:, j * ch:(j + 1) * ch]
            kt_s[slot, khl] = kt

    def advance(items):
        mid = []
        for khl, j, q_r, g_r, gt_r, c, slot in items:
            rows = pl.ds(pl.multiple_of(c * ch, ch), ch)
            _, gl = lanes_of(khl, j)
            gcol = column(g_r[rows, :], gl)
            grow = gt_r[c, pl.ds(gl, 1), :]
            glast = grow[:, ch - 1:ch] if j < rep else grow[:, 0:1]
            sidx = khl * nco + j
            s = st_s[sidx]
            lhs = uw_s[slot, sidx, :, hdim:]
            if q_r is not None:
                qg = q_r[rows, khl * hdim:(khl + 1) * hdim].astype(F32) * jnp.exp(gcol)
                lhs = jnp.concatenate([lhs, qg.astype(BF16)], axis=0)
            mid.append((rows, gcol, glast, sidx, s, _dot(lhs, s.astype(BF16))))
        res = []
        for (khl, j, q_r, g_r, gt_r, c, slot), (rows, gcol, glast, sidx, s, ws) in zip(items, mid):
            vnew = uw_s[slot, sidx, :, :hdim].astype(F32) - ws[:ch]
            snew = s * jnp.exp(glast) + _dot(kt_s[slot, khl], (vnew * jnp.exp(glast - gcol)).astype(BF16))
            o = None if q_r is None else ws[ch:] + _dot(in_s[c, sidx], vnew.astype(BF16))
            res.append((snew, o))
        for (khl, j, q_r, g_r, gt_r, c, slot), (rows, gcol, glast, sidx, s, ws), (snew, o) in zip(items, mid, res):
            st_s[sidx] = snew
            if o is not None:
                v0 = (khl * rep + j % rep) * hdim
                (of_s if j < rep else ob_s)[rows, v0:v0 + hdim] = o.astype(BF16)

    st_s[...] = jnp.zeros(st_s.shape, F32)
    uc = min(unroll, ncc)

    def prep_ctx(i, carry):
        prepare([(khl, kc_ref, None, vc_ref, gc_ref, gct_ref, i * uc + t, i * uc + t)
                 for t in range(uc) for khl in range(khb)])
        return carry

    def prep_lat(i, carry):
        prepare([(khl, k_ref, q_ref, v_ref, g_ref, gt_ref, i * unroll + t, ncc + i * unroll + t)
                 for t in range(unroll) for khl in range(khb)])
        return carry

    def adv_ctx(i, carry):
        advance([(khl, j, None, gc_ref, gct_ref, i if j < rep else ncc - 1 - i, i if j < rep else ncc - 1 - i)
                 for khl in range(khb) for j in range(nco)])
        return carry

    def adv_lat(i, carry):
        advance([(khl, j, q_ref, g_ref, gt_ref, i if j < rep else ncl - 1 - i,
                  ncc + (i if j < rep else ncl - 1 - i)) for khl in range(khb) for j in range(nco)])
        return carry

    lax.fori_loop(0, ncc // uc, prep_ctx, 0)
    lax.fori_loop(0, ncl // unroll, prep_lat, 0)
    lax.fori_loop(0, ncc, adv_ctx, 0)
    lax.fori_loop(0, ncl, adv_lat, 0)

    blk = min(256, ncl * ch)
    ow = ow_ref[...]
    for r0 in range(0, ncl * ch, blk):
        for v0 in range(0, khb * rep * hdim, hdim):
            o = of_s[r0:r0 + blk, v0:v0 + hdim].astype(F32) + ob_s[r0:r0 + blk, v0:v0 + hdim].astype(F32)
            o = o * lax.rsqrt(jnp.mean(o * o, axis=-1, keepdims=True) + NORM_EPS) * ow
            o_ref[r0:r0 + blk, v0:v0 + hdim] = (o * _silu(z_ref[r0:r0 + blk, v0:v0 + hdim].astype(F32))).astype(BF16)


def _gdn_core(p_lat, p_ctx, g_lat, gt_lat, g_ctx, gt_ctx, onorm_w, bsz, hk, hv, hdim, khb=GDN_KEY_HEADS_PER_STEP):
    l = p_lat.shape[0] // bsz
    lc = p_ctx.shape[0] // bsz
    ch = GDN_CHUNK
    ncc, ncl = lc // ch, l // ch
    rep = hv // hk
    nco = 2 * rep
    kw, vw = khb * hdim, khb * rep * hdim
    qk, vd = hk * hdim, hv * hdim
    unroll = min(GDN_PREP_UNROLL, ncl)
    assert hk % khb == 0 and ncl % unroll == 0 and ncc % min(unroll, ncc) == 0
    assert qk % vw == 0 and vd % vw == 0
    kern = functools.partial(_gdn_kernel, hv=hv, khb=khb, rep=rep, hdim=hdim, ncc=ncc, ncl=ncl, unroll=unroll)
    kcol, vcol, zcol = (qk + vd) // kw, (2 * qk + vd) // vw, qk // vw
    cm, bd = _gdn_masks(ch, rep)
    return pl.pallas_call(
        kern,
        grid=(bsz, hk // khb),
        in_specs=[pl.BlockSpec((l, kw), lambda b, h: (b, h)),
                  pl.BlockSpec((l, kw), lambda b, h: (b, kcol + h)),
                  pl.BlockSpec((l, vw), lambda b, h: (b, vcol + h)),
                  pl.BlockSpec((l, vw), lambda b, h: (b, zcol + h)),
                  pl.BlockSpec((lc, kw), lambda b, h: (b, h)),
                  pl.BlockSpec((lc, vw), lambda b, h: (b, qk // vw + h)),
                  pl.BlockSpec((l, LANES), lambda b, h: (b, 0)),
                  pl.BlockSpec((ncl, LANES, ch), lambda b, h: (b, 0, 0)),
                  pl.BlockSpec((lc, LANES), lambda b, h: (b, 0)),
                  pl.BlockSpec((ncc, LANES, ch), lambda b, h: (b, 0, 0)),
                  pl.BlockSpec((1, hdim), lambda b, h: (0, 0)),
                  pl.BlockSpec(cm.shape, lambda b, h: (0, 0, 0)),
                  pl.BlockSpec(bd.shape, lambda b, h: (0, 0))],
        out_specs=pl.BlockSpec((l, vw), lambda b, h: (b, h)),
        out_shape=jax.ShapeDtypeStruct((bsz * l, vd), BF16),
        scratch_shapes=[pltpu.VMEM((l, vw), BF16), pltpu.VMEM((l, vw), BF16),
                        pltpu.VMEM((khb * nco, hdim, hdim), F32),
                        pltpu.VMEM((ncc + ncl, khb * nco, ch, 2 * hdim), BF16),
                        pltpu.VMEM((ncl, khb * nco, ch, ch), BF16),
                        pltpu.VMEM((ncc + ncl, khb, hdim, ch), BF16)],
        compiler_params=_params("parallel", "arbitrary"),
        name="gdn_core",
    )(p_lat, p_lat, p_lat, p_lat, p_ctx, p_ctx, g_lat, gt_lat, g_ctx, gt_ctx, onorm_w.reshape(1, hdim),
      jnp.asarray(cm), jnp.asarray(bd, dtype=BF16))


def kernel(x, c, ctx, c_ctx, ada_w, ada_b, norm1_w, norm2_w, mlp_w1, mlp_w2, s5_lam_re, s5_lam_im, s5_log_dt, s5_b_re, s5_b_im, s5_c_re, s5_c_im, s5_d, s5_w_glu, gdn_w_in, gdn_conv_w, gdn_a_log, gdn_dt_bias, gdn_onorm_w, gdn_w_out, final_norm_w):
    bsz, l, d = x.shape
    lc = ctx.shape[1]
    depth = ada_w.shape[0]
    assert depth == 2 and s5_lam_re.shape[0] == 1 and gdn_w_in.shape[0] == 1
    groups = s5_lam_re.shape[2]
    hv = gdn_a_log.shape[2]
    hdim = gdn_onorm_w.shape[1]
    vd = gdn_w_out.shape[1]
    qk = (gdn_conv_w.shape[2] - vd) // 2
    hk = qk // hdim
    assert l % GRID_W == 0 and l % GDN_CHUNK == 0 and lc % GDN_CHUNK == 0
    assert l % S5_CHUNK == 0 and lc % S5_CHUNK == 0 and 4 * hv <= LANES

    tb_l, tb_c = min(256, l), lc
    tm_l, tm_c = min(512, l), min(512, bsz * lc)
    xl = x.reshape(bsz * l, d)
    xc = ctx.reshape(bsz * lc, d)

    rows = -(-(bsz + 1) // 8) * 8
    cvec = jnp.zeros((rows, d), F32).at[:bsz].set(c).at[bsz].set(c_ctx)
    mod = _ada_mod(cvec, ada_w, ada_b).reshape(depth, rows, 6, d)
    big = 1 << 30

    mod_l, mod_c = mod[0, :bsz], mod[0, bsz:bsz + 1]
    hl = _prenorm(xl, norm1_w[0], mod_l, tb_l, l // tb_l)
    hc = _prenorm(xc, norm1_w[0], mod_c, tb_c, big)
    tables = _s5_tables(s5_lam_re[0], s5_lam_im[0], s5_log_dt[0], s5_b_re[0], s5_b_im[0],
                        s5_c_re[0], s5_c_im[0], s5_d[0], S5_CHUNK)
    yc, yl = _s5_core(_to_groups(hc, bsz, groups, S5_CHUNK), _to_groups(hl, bsz, groups, S5_CHUNK), tables, bsz)
    w_glu = s5_w_glu[0].astype(BF16)
    w1, w2 = mlp_w1[0].astype(BF16), mlp_w2[0].astype(BF16)
    xl = _mixout(_from_groups(yl, bsz, S5_CHUNK), w_glu, xl, mod_l, tm_l, l // tm_l, True)
    xc = _mixout(_from_groups(yc, bsz, S5_CHUNK), w_glu, xc, mod_c, tm_c, big, True)
    xl = _mlp(xl, norm2_w[0], mod_l, w1, w2, final_norm_w, tm_l, l // tm_l, False)
    xc = _mlp(xc, norm2_w[0], mod_c, w1, w2, final_norm_w, tm_c, big, False)

    mod_l, mod_c = mod[1, :bsz], mod[1, bsz:bsz + 1]
    w_in = gdn_w_in[0]
    n_main = 2 * qk + 2 * vd
    w_main = w_in[:, :n_main].astype(BF16)
    w_gate = jnp.pad(w_in[:, n_main:], ((0, 0), (0, LANES - 4 * hv)))
    conv_w = gdn_conv_w[0].astype(F32)
    cw = jnp.concatenate([conv_w[:, :qk], jnp.zeros((conv_w.shape[0], vd), F32), conv_w[:, qk:]], axis=1)
    cw = jnp.pad(cw, ((0, 8 - cw.shape[0]), (0, 0)))
    neg_a = jnp.pad(-jnp.exp(gdn_a_log[0].astype(F32)).reshape(-1), (2 * hv, LANES - 4 * hv))
    dtb = jnp.pad(gdn_dt_bias[0].astype(F32).reshape(-1), (2 * hv, LANES - 4 * hv))
    gp = jnp.stack([neg_a, dtb])
    tn = min(512, qk)
    p_l, g_l, gt_l = _gdn_in(xl, norm1_w[1], mod_l, w_main, w_gate, cw, gp, tb_l, l // tb_l, GRID_W,
                             0, tn, qk, vd, hv, hdim)
    p_c, g_c, gt_c = _gdn_in(xc, norm1_w[1], mod_c, w_main, w_gate, cw, gp, tb_c, big, lc,
                             qk + vd, tn, qk, vd, hv, hdim)
    gated = _gdn_core(p_l, p_c, g_l, gt_l, g_c, gt_c, gdn_onorm_w[0].astype(F32), bsz, hk, hv, hdim)
    xl = _mixout(gated, gdn_w_out[0].astype(BF16), xl, mod_l, tm_l, l // tm_l, False)
    out = _mlp(xl, norm2_w[1], mod_l, mlp_w1[1].astype(BF16), mlp_w2[1].astype(BF16), final_norm_w,
               tm_l, l // tm_l, True)
    return out.reshape(bsz, l, d)
```

```python
import functools
import math

import jax
import jax.numpy as jnp
import numpy as np
from jax import lax
from jax.experimental import pallas as pl
from jax.experimental.pallas import tpu as pltpu

F32 = jnp.float32
BF16 = jnp.bfloat16
HIGHEST = lax.Precision.HIGHEST

NORM_EPS = 1e-6
GRID_W = 64
GDN_CHUNK = 64
GDN_KEY_HEADS_PER_STEP = 2
GDN_PREP_UNROLL = 4
S5_CHUNK = 16
LANES = 128
VMEM_LIMIT = 56 * 1024 * 1024


def _dot(a, b, precision=None):
    return jnp.dot(a, b, preferred_element_type=F32, precision=precision)


def _dot_nt(a, b):
    return lax.dot_general(a, b, (((1,), (1,)), ((), ())), preferred_element_type=F32)


def _dot_tn(a, b):
    return lax.dot_general(a, b, (((0,), (0,)), ((), ())), preferred_element_type=F32)


def _floor_pow2(v, n):
    assert n & (n - 1) == 0
    return jnp.bitwise_and(v, -n)


def _silu(t):
    return t * jax.nn.sigmoid(t)


def _norm_mod(x, nw, shift, scale):
    y = x * lax.rsqrt(jnp.mean(x * x, axis=-1, keepdims=True) + NORM_EPS) * nw
    return y * (1.0 + scale) + shift


def _params(*sem):
    return pltpu.CompilerParams(dimension_semantics=sem, vmem_limit_bytes=VMEM_LIMIT)


def _ada_kernel(c_ref, w_ref, b_ref, o_ref):
    o_ref[0] = _dot(_silu(c_ref[...]), w_ref[0], HIGHEST) + b_ref[0]


def _ada_mod(cvec, ada_w, ada_b):
    depth, d, n = ada_w.shape
    rows = cvec.shape[0]
    tn = n // 4
    return pl.pallas_call(
        _ada_kernel,
        grid=(depth, n // tn),
        in_specs=[pl.BlockSpec((rows, d), lambda i, j: (0, 0)),
                  pl.BlockSpec((1, d, tn), lambda i, j: (i, 0, j)),
                  pl.BlockSpec((1, 1, tn), lambda i, j: (i, 0, j))],
        out_specs=pl.BlockSpec((1, rows, tn), lambda i, j: (i, 0, j)),
        out_shape=jax.ShapeDtypeStruct((depth, rows, n), F32),
        compiler_params=_params("arbitrary", "arbitrary"),
        name="ada_mod",
    )(cvec, ada_w, ada_b.reshape(depth, 1, n))


def _prenorm_kernel(x_ref, nw_ref, mod_ref, o_ref):
    h = _norm_mod(x_ref[...], nw_ref[...], mod_ref[0, 0:1, :], mod_ref[0, 1:2, :])
    o_ref[...] = h.astype(o_ref.dtype)


def _prenorm(x, nw, mod, tile, tiles_per_mod):
    n, d = x.shape
    return pl.pallas_call(
        _prenorm_kernel,
        grid=(n // tile,),
        in_specs=[pl.BlockSpec((tile, d), lambda i: (i, 0)),
                  pl.BlockSpec((1, d), lambda i: (0, 0)),
                  pl.BlockSpec((1, 6, d), lambda i: (i // tiles_per_mod, 0, 0))],
        out_specs=pl.BlockSpec((tile, d), lambda i: (i, 0)),
        out_shape=jax.ShapeDtypeStruct((n, d), BF16),
        compiler_params=_params("parallel"),
        name="prenorm",
    )(x, nw.reshape(1, d), mod)


def _s5_tables(lam_re, lam_im, log_dt, b_re, b_im, c_re, c_im, d_skip, t):
    _, g, p = lam_re.shape
    h = b_re.shape[-1]
    f = lambda a: a.astype(F32)
    lam_re, lam_im, b_re, b_im, c_re, c_im = map(f, (lam_re, lam_im, b_re, b_im, c_re, c_im))
    dt = jnp.exp(f(log_dt))[..., None]
    zr, zi = lam_re * dt, lam_im * dt
    n = jnp.arange(t + 1, dtype=F32)[:, None, None, None]
    mag = jnp.exp(n * zr)
    pr, pi = mag * jnp.cos(n * zi), mag * jnp.sin(n * zi)
    ar, ai = pr[1], pi[1]
    den = lam_re * lam_re + lam_im * lam_im
    fr = ((ar - 1.0) * lam_re + ai * lam_im) / den
    fi = (ai * lam_re - (ar - 1.0) * lam_im) / den
    bbr = fr[..., None] * b_re - fi[..., None] * b_im
    bbi = fr[..., None] * b_im + fi[..., None] * b_re
    car = c_re[None] * pr[:, :, :, None, :] - c_im[None] * pi[:, :, :, None, :]
    cai = c_re[None] * pi[:, :, :, None, :] + c_im[None] * pr[:, :, :, None, :]
    kern = (jnp.einsum('ndgop,dgpi->ndgoi', car, bbr, precision=HIGHEST)
            - jnp.einsum('ndgop,dgpi->ndgoi', cai, bbi, precision=HIGHEST))
    ti = jnp.arange(t)
    lag = ti[None, :] - ti[:, None]
    kf = jnp.where((lag >= 0)[:, :, None, None, None], kern[jnp.clip(lag, 0, t), 0], 0.0)
    kb = jnp.where((lag <= 0)[:, :, None, None, None], kern[jnp.clip(-lag, 0, t), 1], 0.0)
    intra = jnp.transpose(kf + kb, (2, 0, 4, 1, 3)).reshape(g, t * h, t * h)

    def inject(d, powers):
        er, ei = pr[powers, d], pi[powers, d]
        re = er[..., None] * bbr[d][None] - ei[..., None] * bbi[d][None]
        im = er[..., None] * bbi[d][None] + ei[..., None] * bbr[d][None]
        tr = lambda a: jnp.transpose(a, (1, 0, 3, 2)).reshape(g, t * h, p)
        return tr(re), tr(im)

    pad = lambda a: jnp.pad(a, ((0, 0), (0, 0), (0, LANES - p)))
    in_f = inject(0, t - 1 - ti)
    in_b = inject(1, ti)
    w1 = jnp.concatenate([intra] + [pad(a) for a in (*in_f, *in_b)], axis=-1)

    def readout(d, powers):
        tr = lambda a: jnp.transpose(a, (1, 3, 0, 2)).reshape(g, p, t * h)
        return tr(car[powers, d]), tr(-cai[powers, d])

    padr = lambda a: jnp.pad(a, ((0, 0), (0, LANES - p), (0, 0)))
    w2 = jnp.concatenate([padr(a) for a in (*readout(0, ti + 1), *readout(1, t - ti))], axis=1)
    dec = jnp.stack([pr[t, 0], pi[t, 0], pr[t, 1], pi[t, 1]], axis=1)
    dec = jnp.pad(dec, ((0, 0), (0, 4), (0, LANES - p)))
    dsk = jnp.tile(f(d_skip).reshape(g, 1, h), (1, t, 1)).reshape(g, 1, t * h)
    return w1.astype(BF16), w2.astype(BF16), dec, dsk


def _s5_kernel(hc_ref, hl_ref, pg_ref, pt_ref, w1_ref, w2_ref, dec_ref, dsk_ref, yc_ref, yl_ref,
               z_scr, h_scr, u_scr, *, bsz, ncc, ncl, width, row_blk):
    rc, rl = ncc * bsz, ncl * bsz
    t = hl_ref.shape[0]
    cb = max(1, row_blk // bsz)
    w1 = w1_ref[0]
    pg = pg_ref[0]

    @pl.when(pl.program_id(1) == 0)
    def _():
        yc_ref[...] = jnp.zeros(yc_ref.shape, BF16)
        yl_ref[...] = jnp.zeros(yl_ref.shape, BF16)

    def blocks():
        for h_ref, y_ref, base, nch in ((hc_ref, yc_ref, 0, ncc), (hl_ref, yl_ref, rc, ncl)):
            for c0 in range(0, nch, cb):
                n = min(cb, nch - c0)
                yield h_ref, y_ref, c0, n, slice(base + c0 * bsz, base + (c0 + n) * bsz)

    for h_ref, _, c0, n, sl in blocks():
        xcat = jnp.concatenate([h_ref[tt, c0:c0 + n].reshape(n * bsz, LANES) for tt in range(t)], axis=1)
        u = _dot(xcat, pg).astype(BF16)
        u_scr[sl, :] = u
        z_scr[sl, :] = _dot(u, w1)
    dec = dec_ref[0]
    zero = jnp.zeros((bsz, LANES), F32)
    cols = [width + k * LANES for k in range(5)]

    def make_step(are, aim, col, hcol):
        def step(row, carry):
            re, im = carry
            r = pl.multiple_of(row, bsz)
            h_scr[pl.ds(r, bsz), hcol:hcol + LANES] = re
            h_scr[pl.ds(r, bsz), hcol + LANES:hcol + 2 * LANES] = im
            s_re = z_scr[pl.ds(r, bsz), cols[col]:cols[col + 1]]
            s_im = z_scr[pl.ds(r, bsz), cols[col + 1]:cols[col + 2]]
            return are * re - aim * im + s_re, are * im + aim * re + s_im
        return step

    fstep = make_step(dec[0:1, :], dec[1:2, :], 0, 0)
    bstep = make_step(dec[2:3, :], dec[3:4, :], 2, 2 * LANES)
    lax.fori_loop(0, ncc + ncl, lambda c, s: fstep(c * bsz, s), (zero, zero))
    st = lax.fori_loop(0, ncc, lambda i, s: bstep((ncc - 1 - i) * bsz, s), (zero, zero))
    lax.fori_loop(0, ncl, lambda i, s: bstep((ncc + ncl - 1 - i) * bsz, s), st)

    w2 = w2_ref[0]
    dsk = dsk_ref[0]
    pt = pt_ref[0]
    for _, y_ref, c0, n, sl in blocks():
        y = z_scr[sl, 0:width] + _dot(h_scr[sl, :].astype(BF16), w2)
        y = jax.nn.gelu(y + dsk * u_scr[sl, :].astype(F32)).astype(BF16)
        yn = _dot(y, pt)
        for tt in range(t):
            slab = yn[:, tt * LANES:(tt + 1) * LANES].astype(BF16).reshape(n, bsz, LANES)
            y_ref[tt, c0:c0 + n] = y_ref[tt, c0:c0 + n] + slab


def _s5_lane_perm(t, hch):
    ng = LANES // hch
    pg = np.zeros((ng, t, LANES, t, hch), np.float32)
    for g in range(ng):
        for tt in range(t):
            pg[g, tt, g * hch + np.arange(hch), tt, np.arange(hch)] = 1.0
    return pg.reshape(ng, t * LANES, t * hch)


def _s5_core(hc, hl, tables, row_blk=512):
    w1, w2, dec, dsk = tables
    t, ncc, bsz, d = hc.shape
    ncl = hl.shape[1]
    width = w2.shape[2]
    hch = width // t
    ng = LANES // hch
    rows = (ncc + ncl) * bsz
    pg = _s5_lane_perm(t, hch)
    kern = functools.partial(_s5_kernel, bsz=bsz, ncc=ncc, ncl=ncl, width=width, row_blk=row_blk)
    gmap = lambda j, q: (j * ng + q, 0, 0)
    qmap = lambda j, q: (q, 0, 0)
    once = pl.Buffered(1)
    return pl.pallas_call(
        kern,
        grid=(d // LANES, ng),
        in_specs=[pl.BlockSpec((t, ncc, bsz, LANES), lambda j, q: (0, 0, 0, j), pipeline_mode=once),
                  pl.BlockSpec((t, ncl, bsz, LANES), lambda j, q: (0, 0, 0, j), pipeline_mode=once),
                  pl.BlockSpec((1,) + pg.shape[1:], qmap), pl.BlockSpec((1, t * hch, t * LANES), qmap),
                  pl.BlockSpec((1,) + w1.shape[1:], gmap), pl.BlockSpec((1,) + w2.shape[1:], gmap),
                  pl.BlockSpec((1,) + dec.shape[1:], gmap), pl.BlockSpec((1,) + dsk.shape[1:], gmap)],
        out_specs=[pl.BlockSpec((t, ncc, bsz, LANES), lambda j, q: (0, 0, 0, j), pipeline_mode=once),
                   pl.BlockSpec((t, ncl, bsz, LANES), lambda j, q: (0, 0, 0, j), pipeline_mode=once)],
        out_shape=[jax.ShapeDtypeStruct(hc.shape, BF16), jax.ShapeDtypeStruct(hl.shape, BF16)],
        scratch_shapes=[pltpu.VMEM((rows, w1.shape[2]), F32), pltpu.VMEM((rows, 4 * LANES), F32),
                        pltpu.VMEM((rows, width), BF16)],
        compiler_params=_params("parallel", "arbitrary"),
        name="s5_core",
    )(hc, hl, jnp.asarray(pg, dtype=BF16), jnp.asarray(pg.transpose(0, 2, 1), dtype=BF16), w1, w2, dec, dsk)


def _time_major(hm, bsz, t):
    n, d = hm.shape
    return jnp.transpose(hm.reshape(bsz, n // bsz // t, t, d), (2, 1, 0, 3))


def _token_major(y):
    t, nc, bsz, d = y.shape
    return jnp.transpose(y, (2, 1, 0, 3)).reshape(bsz * nc * t, d)


def _mixout_kernel(y_ref, w_ref, x_ref, mod_ref, o_ref, *, glu):
    r = _dot(y_ref[...], w_ref[...])
    if glu:
        half = r.shape[-1] // 2
        r = r[:, :half] * jax.nn.sigmoid(r[:, half:])
    o_ref[...] = x_ref[...] + mod_ref[0, 2:3, :] * r


def _mixout(y, w, x, mod, tile, tiles_per_mod, glu):
    n, d = x.shape
    k, nn = w.shape
    return pl.pallas_call(
        functools.partial(_mixout_kernel, glu=glu),
        grid=(n // tile,),
        in_specs=[pl.BlockSpec((tile, k), lambda i: (i, 0)),
                  pl.BlockSpec((k, nn), lambda i: (0, 0)),
                  pl.BlockSpec((tile, d), lambda i: (i, 0)),
                  pl.BlockSpec((1, 6, d), lambda i: (i // tiles_per_mod, 0, 0))],
        out_specs=pl.BlockSpec((tile, d), lambda i: (i, 0)),
        out_shape=jax.ShapeDtypeStruct((n, d), F32),
        compiler_params=_params("parallel"),
        name="mixer_out",
    )(y, w, x, mod)


def _mlp_kernel(x_ref, nw_ref, mod_ref, w1_ref, w2_ref, fw_ref, o_ref, *, ff_blk, final):
    x = x_ref[...]
    h = _norm_mod(x, nw_ref[...], mod_ref[0, 3:4, :], mod_ref[0, 4:5, :]).astype(BF16)
    acc = jnp.zeros(x.shape, F32)
    for f0 in range(0, w1_ref.shape[1], ff_blk):
        a = jnp.maximum(_dot(h, w1_ref[:, f0:f0 + ff_blk]), 0.0)
        acc = acc + _dot((a * a).astype(BF16), w2_ref[f0:f0 + ff_blk, :])
    y = x + mod_ref[0, 5:6, :] * acc
    if final:
        y = y * lax.rsqrt(jnp.mean(y * y, axis=-1, keepdims=True) + NORM_EPS) * fw_ref[...]
    o_ref[...] = y


def _mlp(x, nw, mod, w1, w2, fw, tile, tiles_per_mod, final):
    n, d = x.shape
    ff = w1.shape[1]
    return pl.pallas_call(
        functools.partial(_mlp_kernel, ff_blk=min(1024, ff), final=final),
        grid=(n // tile,),
        in_specs=[pl.BlockSpec((tile, d), lambda i: (i, 0)),
                  pl.BlockSpec((1, d), lambda i: (0, 0)),
                  pl.BlockSpec((1, 6, d), lambda i: (i // tiles_per_mod, 0, 0)),
                  pl.BlockSpec((d, ff), lambda i: (0, 0)),
                  pl.BlockSpec((ff, d), lambda i: (0, 0)),
                  pl.BlockSpec((1, d), lambda i: (0, 0))],
        out_specs=pl.BlockSpec((tile, d), lambda i: (i, 0)),
        out_shape=jax.ShapeDtypeStruct((n, d), F32),
        compiler_params=_params("parallel"),
        name="mlp",
    )(x, nw.reshape(1, d), mod, w1, w2, fw.reshape(1, d))


def _gdn_in_kernel(x_ref, nw_ref, mod_ref, w_ref, wg_ref, cw_ref, gp_ref, p_ref, g_ref, gt_ref,
                   *, period, col0, qk, vd, tn, hv, hdim):
    tb = x_ref.shape[0]
    h = _norm_mod(x_ref[...], nw_ref[...], mod_ref[0, 0:1, :], mod_ref[0, 1:2, :])
    hb = h.astype(BF16)
    gl = _dot(h, wg_ref[...], HIGHEST)
    lane = lax.broadcasted_iota(jnp.int32, (1, LANES), 1)
    beta = jax.nn.sigmoid(gl)
    t = gl + gp_ref[1:2, :]
    g = gp_ref[0:1, :] * (jnp.maximum(t, 0.0) + jnp.log1p(jnp.exp(-jnp.abs(t))))
    r = lax.broadcasted_iota(jnp.int32, (tb, tb), 0)
    c = lax.broadcasted_iota(jnp.int32, (tb, tb), 1)
    same = _floor_pow2(r, GDN_CHUNK) == _floor_pow2(c, GDN_CHUNK)
    gcf = _dot(jnp.where(same & (c <= r), 1.0, 0.0), g, HIGHEST)
    gcb = _dot(jnp.where(same & (c >= r), 1.0, 0.0), g, HIGHEST)
    out = jnp.where(lane < 2 * hv, beta, jnp.where(lane < 3 * hv, gcf, jnp.where(lane < 4 * hv, gcb, 0.0)))
    g_ref[...] = out
    for cc in range(tb // GDN_CHUNK):
        gt_ref[cc] = out[cc * GDN_CHUNK:(cc + 1) * GDN_CHUNK, :].T

    row = lax.broadcasted_iota(jnp.int32, (tb, 1), 0)
    pos = row - _floor_pow2(row, period)
    taps = [(tap, ((pos + tap - 2 >= 0) & (pos + tap - 2 < period)).astype(F32)) for tap in (0, 1, 3, 4)]

    def conv_silu(a, cw):
        y = a * cw[2:3, :]
        for tap, ok in taps:
            sh = pltpu.roll(a, (tb - (tap - 2)) % tb, 0)
            y = y + sh * (ok * cw[tap:tap + 1, :])
        return _silu(y)

    def l2n(y, scale):
        outs = []
        for h0 in range(0, tn, hdim):
            yh = y[:, h0:h0 + hdim]
            outs.append(yh * (lax.rsqrt(jnp.sum(yh * yh, axis=-1, keepdims=True) + NORM_EPS) * scale))
        return jnp.concatenate(outs, axis=-1) if len(outs) > 1 else outs[0]

    for o0 in range(0, p_ref.shape[1], tn):
        c0 = col0 + o0
        acc = _dot(hb, w_ref[:, c0:c0 + tn])
        if c0 < qk:
            res = l2n(conv_silu(acc, cw_ref[:, c0:c0 + tn]), hdim ** -0.5)
        elif c0 < qk + vd:
            res = acc
        elif c0 < 2 * qk + vd:
            res = l2n(conv_silu(acc, cw_ref[:, c0:c0 + tn]), 1.0)
        else:
            res = conv_silu(acc, cw_ref[:, c0:c0 + tn])
        p_ref[:, o0:o0 + tn] = res.astype(BF16)


def _gdn_in(x, nw, mod, w_main, w_gate, cw, gp, tile, tiles_per_mod, period, col0, tn, qk, vd, hv, hdim):
    n, d = x.shape
    n_main = w_main.shape[1]
    kern = functools.partial(_gdn_in_kernel, period=period, col0=col0, qk=qk, vd=vd, tn=tn, hv=hv, hdim=hdim)
    return pl.pallas_call(
        kern,
        grid=(n // tile,),
        in_specs=[pl.BlockSpec((tile, d), lambda i: (i, 0)),
                  pl.BlockSpec((1, d), lambda i: (0, 0)),
                  pl.BlockSpec((1, 6, d), lambda i: (i // tiles_per_mod, 0, 0)),
                  pl.BlockSpec((d, n_main), lambda i: (0, 0)),
                  pl.BlockSpec((d, LANES), lambda i: (0, 0)),
                  pl.BlockSpec((8, n_main), lambda i: (0, 0)),
                  pl.BlockSpec((2, LANES), lambda i: (0, 0))],
        out_specs=[pl.BlockSpec((tile, n_main - col0), lambda i: (i, 0)),
                   pl.BlockSpec((tile, LANES), lambda i: (i, 0)),
                   pl.BlockSpec((tile // GDN_CHUNK, LANES, GDN_CHUNK), lambda i: (i, 0, 0))],
        out_shape=[jax.ShapeDtypeStruct((n, n_main - col0), BF16),
                   jax.ShapeDtypeStruct((n, LANES), F32),
                   jax.ShapeDtypeStruct((n // GDN_CHUNK, LANES, GDN_CHUNK), F32)],
        compiler_params=_params("parallel"),
        name="gdn_in",
    )(x, nw.reshape(1, d), mod, w_main, w_gate, cw, gp)


def _gdn_masks(ch, rep):
    nco = 2 * rep
    r, c = np.meshgrid(np.arange(ch), np.arange(ch), indexing="ij")
    per_dir = []
    for lower in (True, False):
        m = [(c <= r) if lower else (c >= r), (c < r) if lower else (c > r)]
        b = 1
        while b < ch:
            same = (r // (2 * b)) == (c // (2 * b))
            hi_r, hi_c = (r % (2 * b)) >= b, (c % (2 * b)) >= b
            m.append(same & ((hi_r & ~hi_c) if lower else (~hi_r & hi_c)))
            b *= 2
        per_dir.append(np.stack(m))
    cm = np.concatenate([per_dir[j // rep] for j in range(nco)], axis=2).astype(np.float32)
    blk = np.arange(nco * ch) // ch
    return cm, (blk[:, None] == blk[None, :]).astype(np.float32)


def _gdn_kernel(q_ref, k_ref, v_ref, z_ref, kc_ref, vc_ref, g_ref, gt_ref, gc_ref, gct_ref, ow_ref, cm_ref, bd_ref,
                o_ref, of_s, ob_s, st_s, uw_s, in_s, kt_s, *, hv, khb, rep, hdim, ncc, ncl, unroll):
    ch = GDN_CHUNK
    nco = 2 * rep
    nlvl = cm_ref.shape[0] - 2
    kh0 = pl.program_id(1) * khb
    lane = lax.broadcasted_iota(jnp.int32, (1, LANES), 1)
    lcat = lax.broadcasted_iota(jnp.int32, (1, nco * ch), 1)

    def lanes_of(khl, j):
        head = (kh0 + khl) * rep + j % rep
        return (j // rep) * hv + head, (2 + j // rep) * hv + head

    def column(gt, idx):
        return jnp.sum(jnp.where(lane == idx, gt, 0.0), axis=1, keepdims=True)

    def spread(cols):
        out = jnp.broadcast_to(cols[nco - 1], (ch, nco * ch))
        for j in range(nco - 2, -1, -1):
            out = jnp.where(lcat < (j + 1) * ch, cols[j], out)
        return out

    def block_diag(xb):
        return jnp.concatenate([xb] * nco, axis=0) * bd_ref[...]

    def prepare(insts):
        st = []
        for khl, k_r, q_r, v_r, g_r, gt_r, c, slot in insts:
            rows = pl.ds(pl.multiple_of(c * ch, ch), ch)
            kb = k_r[rows, khl * hdim:(khl + 1) * hdim]
            kcat = jnp.concatenate([kb] * nco, axis=0)
            if q_r is None:
                kkc, qkc = _dot_nt(kb, kcat), None
            else:
                prod = _dot_nt(jnp.concatenate([kb, q_r[rows, khl * hdim:(khl + 1) * hdim]], axis=0), kcat)
                kkc, qkc = prod[:ch], prod[ch:]
            gtile = g_r[rows, :]
            betas, gcols, grows = [], [], []
            for j in range(nco):
                bl, gl = lanes_of(khl, j)
                betas.append(column(gtile, bl))
                gcols.append(column(gtile, gl))
                grows.append(gt_r[c, pl.ds(gl, 1), :])
            dlog = spread(gcols) - jnp.concatenate(grows, axis=1)
            decay = jnp.exp(jnp.where(cm_ref[0] > 0.0, dlog, -jnp.inf))
            ac = spread(betas) * kkc * decay * cm_ref[1]
            xc = (cm_ref[0] - cm_ref[1]) - ac * cm_ref[2]
            st.append([rows, kb, qkc, betas, gcols, decay, ac, xc])
        for lvl in range(1, nlvl):
            xbs = [s[7].astype(BF16) for s in st]
            ps = [_dot((s[6] * cm_ref[2 + lvl]).astype(BF16), block_diag(xb)) for s, xb in zip(st, xbs)]
            for s, xb, p in zip(st, xbs, ps):
                s[7] = s[7] - _dot(xb, block_diag(p.astype(BF16)))
        outs = []
        for (khl, k_r, q_r, v_r, g_r, gt_r, c, slot), (rows, kb, qkc, betas, gcols, decay, ac, xc) in zip(insts, st):
            kf = kb.astype(F32)
            rst = []
            for j in range(nco):
                v0 = (khl * rep + j % rep) * hdim
                vf = v_r[rows, v0:v0 + hdim].astype(F32)
                rst.append(jnp.concatenate([vf * betas[j], kf * (betas[j] * jnp.exp(gcols[j]))], axis=1))
            uw = _dot(block_diag(xc.astype(BF16)), jnp.concatenate(rst, axis=0).astype(BF16))
            intra = None if qkc is None else (qkc * decay * cm_ref[0]).astype(BF16)
            outs.append((uw, kf.T.astype(BF16), intra))
        for (khl, k_r, q_r, v_r, g_r, gt_r, c, slot), (uw, kt, intra) in zip(insts, outs):
            for j in range(nco):
                uw_s[slot, khl * nco + j] = uw[j * ch:(j + 1) * ch, :].astype(BF16)
                if intra is not None:
                    in_s[c, khl * nco + j] = intra[:, j * ch:(j + 1) * ch]
            kt_s[slot, khl] = kt

    def advance(items):
        mid = []
        for khl, j, q_r, g_r, gt_r, c, slot in items:
            rows = pl.ds(pl.multiple_of(c * ch, ch), ch)
            _, gl = lanes_of(khl, j)
            gcol = column(g_r[rows, :], gl)
            grow = gt_r[c, pl.ds(gl, 1), :]
            glast = grow[:, ch - 1:ch] if j < rep else grow[:, 0:1]
            sidx = khl * nco + j
            s = st_s[sidx]
            lhs = uw_s[slot, sidx, :, hdim:]
            if q_r is not None:
                qg = q_r[rows, khl * hdim:(khl + 1) * hdim].astype(F32) * jnp.exp(gcol)
                lhs = jnp.concatenate([lhs, qg.astype(BF16)], axis=0)
            mid.append((rows, gcol, glast, sidx, s, _dot(lhs, s.astype(BF16))))
        res = []
        for (khl, j, q_r, g_r, gt_r, c, slot), (rows, gcol, glast, sidx, s, ws) in zip(items, mid):
            vnew = uw_s[slot, sidx, :, :hdim].astype(F32) - ws[:ch]
            snew = s * jnp.exp(glast) + _dot(kt_s[slot, khl], (vnew * jnp.exp(glast - gcol)).astype(BF16))
            o = None if q_r is None else ws[ch:] + _dot(in_s[c, sidx], vnew.astype(BF16))
            res.append((snew, o))
        for (khl, j, q_r, g_r, gt_r, c, slot), (rows, gcol, glast, sidx, s, ws), (snew, o) in zip(items, mid, res):
            st_s[sidx] = snew
            if o is not None:
                v0 = (khl * rep + j % rep) * hdim
                (of_s if j < rep else ob_s)[rows, v0:v0 + hdim] = o.astype(BF16)

    st_s[...] = jnp.zeros(st_s.shape, F32)
    uc = min(unroll, ncc)

    def prep_ctx(i, carry):
        prepare([(khl, kc_ref, None, vc_ref, gc_ref, gct_ref, i * uc + t, i * uc + t)
                 for t in range(uc) for khl in range(khb)])
        return carry

    def prep_lat(i, carry):
        prepare([(khl, k_ref, q_ref, v_ref, g_ref, gt_ref, i * unroll + t, ncc + i * unroll + t)
                 for t in range(unroll) for khl in range(khb)])
        return carry

    def adv_ctx(i, carry):
        advance([(khl, j, None, gc_ref, gct_ref, i if j < rep else ncc - 1 - i, i if j < rep else ncc - 1 - i)
                 for khl in range(khb) for j in range(nco)])
        return carry

    def adv_lat(i, carry):
        advance([(khl, j, q_ref, g_ref, gt_ref, i if j < rep else ncl - 1 - i,
                  ncc + (i if j < rep else ncl - 1 - i)) for khl in range(khb) for j in range(nco)])
        return carry

    lax.fori_loop(0, ncc // uc, prep_ctx, 0)
    lax.fori_loop(0, ncl // unroll, prep_lat, 0)
    lax.fori_loop(0, ncc, adv_ctx, 0)
    lax.fori_loop(0, ncl, adv_lat, 0)

    blk = min(256, ncl * ch)
    ow = ow_ref[...]
    for r0 in range(0, ncl * ch, blk):
        for v0 in range(0, khb * rep * hdim, hdim):
            o = of_s[r0:r0 + blk, v0:v0 + hdim].astype(F32) + ob_s[r0:r0 + blk, v0:v0 + hdim].astype(F32)
            o = o * lax.rsqrt(jnp.mean(o * o, axis=-1, keepdims=True) + NORM_EPS) * ow
            o_ref[r0:r0 + blk, v0:v0 + hdim] = (o * _silu(z_ref[r0:r0 + blk, v0:v0 + hdim].astype(F32))).astype(BF16)


def _gdn_core(p_lat, p_ctx, g_lat, gt_lat, g_ctx, gt_ctx, onorm_w, bsz, hk, hv, hdim, khb=GDN_KEY_HEADS_PER_STEP):
    l = p_lat.shape[0] // bsz
    lc = p_ctx.shape[0] // bsz
    ch = GDN_CHUNK
    ncc, ncl = lc // ch, l // ch
    rep = hv // hk
    nco = 2 * rep
    kw, vw = khb * hdim, khb * rep * hdim
    qk, vd = hk * hdim, hv * hdim
    unroll = min(GDN_PREP_UNROLL, ncl)
    assert hk % khb == 0 and ncl % unroll == 0 and ncc % min(unroll, ncc) == 0
    assert qk % vw == 0 and vd % vw == 0
    kern = functools.partial(_gdn_kernel, hv=hv, khb=khb, rep=rep, hdim=hdim, ncc=ncc, ncl=ncl, unroll=unroll)
    kcol, vcol, zcol = (qk + vd) // kw, (2 * qk + vd) // vw, qk // vw
    cm, bd = _gdn_masks(ch, rep)
    return pl.pallas_call(
        kern,
        grid=(bsz, hk // khb),
        in_specs=[pl.BlockSpec((l, kw), lambda b, h: (b, h)),
                  pl.BlockSpec((l, kw), lambda b, h: (b, kcol + h)),
                  pl.BlockSpec((l, vw), lambda b, h: (b, vcol + h)),
                  pl.BlockSpec((l, vw), lambda b, h: (b, zcol + h)),
                  pl.BlockSpec((lc, kw), lambda b, h: (b, h)),
                  pl.BlockSpec((lc, vw), lambda b, h: (b, qk // vw + h)),
                  pl.BlockSpec((l, LANES), lambda b, h: (b, 0)),
                  pl.BlockSpec((ncl, LANES, ch), lambda b, h: (b, 0, 0)),
                  pl.BlockSpec((lc, LANES), lambda b, h: (b, 0)),
                  pl.BlockSpec((ncc, LANES, ch), lambda b, h: (b, 0, 0)),
                  pl.BlockSpec((1, hdim), lambda b, h: (0, 0)),
                  pl.BlockSpec(cm.shape, lambda b, h: (0, 0, 0)),
                  pl.BlockSpec(bd.shape, lambda b, h: (0, 0))],
        out_specs=pl.BlockSpec((l, vw), lambda b, h: (b, h)),
        out_shape=jax.ShapeDtypeStruct((bsz * l, vd), BF16),
        scratch_shapes=[pltpu.VMEM((l, vw), BF16), pltpu.VMEM((l, vw), BF16),
                        pltpu.VMEM((khb * nco, hdim, hdim), F32),
                        pltpu.VMEM((ncc + ncl, khb * nco, ch, 2 * hdim), BF16),
                        pltpu.VMEM((ncl, khb * nco, ch, ch), BF16),
                        pltpu.VMEM((ncc + ncl, khb, hdim, ch), BF16)],
        compiler_params=_params("parallel", "arbitrary"),
        name="gdn_core",
    )(p_lat, p_lat, p_lat, p_lat, p_ctx, p_ctx, g_lat, gt_lat, g_ctx, gt_ctx, onorm_w.reshape(1, hdim),
      jnp.asarray(cm), jnp.asarray(bd, dtype=BF16))


def kernel(x, c, ctx, c_ctx, ada_w, ada_b, norm1_w, norm2_w, mlp_w1, mlp_w2, s5_lam_re, s5_lam_im, s5_log_dt, s5_b_re, s5_b_im, s5_c_re, s5_c_im, s5_d, s5_w_glu, gdn_w_in, gdn_conv_w, gdn_a_log, gdn_dt_bias, gdn_onorm_w, gdn_w_out, final_norm_w):
    bsz, l, d = x.shape
    lc = ctx.shape[1]
    depth = ada_w.shape[0]
    assert depth == 2 and s5_lam_re.shape[0] == 1 and gdn_w_in.shape[0] == 1
    groups = s5_lam_re.shape[2]
    hv = gdn_a_log.shape[2]
    hdim = gdn_onorm_w.shape[1]
    vd = gdn_w_out.shape[1]
    qk = (gdn_conv_w.shape[2] - vd) // 2
    hk = qk // hdim
    assert l % GRID_W == 0 and l % GDN_CHUNK == 0 and lc % GDN_CHUNK == 0
    assert l % S5_CHUNK == 0 and lc % S5_CHUNK == 0 and 4 * hv <= LANES

    tb_l, tb_c = min(256, l), lc
    tm_l, tm_c = min(512, l), min(512, bsz * lc)
    xl = x.reshape(bsz * l, d)
    xc = ctx.reshape(bsz * lc, d)

    rows = -(-(bsz + 1) // 8) * 8
    cvec = jnp.zeros((rows, d), F32).at[:bsz].set(c).at[bsz].set(c_ctx)
    mod = _ada_mod(cvec, ada_w, ada_b).reshape(depth, rows, 6, d)
    big = 1 << 30

    mod_l, mod_c = mod[0, :bsz], mod[0, bsz:bsz + 1]
    hl = _prenorm(xl, norm1_w[0], mod_l, tb_l, l // tb_l)
    hc = _prenorm(xc, norm1_w[0], mod_c, tb_c, big)
    tables = _s5_tables(s5_lam_re[0], s5_lam_im[0], s5_log_dt[0], s5_b_re[0], s5_b_im[0],
                        s5_c_re[0], s5_c_im[0], s5_d[0], S5_CHUNK)
    yc, yl = _s5_core(_time_major(hc, bsz, S5_CHUNK), _time_major(hl, bsz, S5_CHUNK), tables)
    w_glu = s5_w_glu[0].astype(BF16)
    w1, w2 = mlp_w1[0].astype(BF16), mlp_w2[0].astype(BF16)
    xl = _mixout(_token_major(yl), w_glu, xl, mod_l, tm_l, l // tm_l, True)
    xc = _mixout(_token_major(yc), w_glu, xc, mod_c, tm_c, big, True)
    xl = _mlp(xl, norm2_w[0], mod_l, w1, w2, final_norm_w, tm_l, l // tm_l, False)
    xc = _mlp(xc, norm2_w[0], mod_c, w1, w2, final_norm_w, tm_c, big, False)

    mod_l, mod_c = mod[1, :bsz], mod[1, bsz:bsz + 1]
    w_in = gdn_w_in[0]
    n_main = 2 * qk + 2 * vd
    w_main = w_in[:, :n_main].astype(BF16)
    w_gate = jnp.pad(w_in[:, n_main:], ((0, 0), (0, LANES - 4 * hv)))
    conv_w = gdn_conv_w[0].astype(F32)
    cw = jnp.concatenate([conv_w[:, :qk], jnp.zeros((conv_w.shape[0], vd), F32), conv_w[:, qk:]], axis=1)
    cw = jnp.pad(cw, ((0, 8 - cw.shape[0]), (0, 0)))
    neg_a = jnp.pad(-jnp.exp(gdn_a_log[0].astype(F32)).reshape(-1), (2 * hv, LANES - 4 * hv))
    dtb = jnp.pad(gdn_dt_bias[0].astype(F32).reshape(-1), (2 * hv, LANES - 4 * hv))
    gp = jnp.stack([neg_a, dtb])
    tn = min(512, qk)
    p_l, g_l, gt_l = _gdn_in(xl, norm1_w[1], mod_l, w_main, w_gate, cw, gp, tb_l, l // tb_l, GRID_W,
                             0, tn, qk, vd, hv, hdim)
    p_c, g_c, gt_c = _gdn_in(xc, norm1_w[1], mod_c, w_main, w_gate, cw, gp, tb_c, big, lc,
                             qk + vd, tn, qk, vd, hv, hdim)
    gated = _gdn_core(p_l, p_c, g_l, gt_l, g_c, gt_c, gdn_onorm_w[0].astype(F32), bsz, hk, hv, hdim)
    xl = _mixout(gated, gdn_w_out[0].astype(BF16), xl, mod_l, tm_l, l // tm_l, False)
    out = _mlp(xl, norm2_w[1], mod_l, mlp_w1[1].astype(BF16), mlp_w2[1].astype(BF16), final_norm_w,
               tm_l, l // tm_l, True)
    return out.reshape(bsz, l, d)
```

```python
import functools
import math

import jax
import jax.numpy as jnp
import numpy as np
from jax import lax
from jax.experimental import pallas as pl
from jax.experimental.pallas import tpu as pltpu

F32 = jnp.float32
BF16 = jnp.bfloat16
HIGHEST = lax.Precision.HIGHEST

NORM_EPS = 1e-6
GRID_W = 64
GDN_CHUNK = 64
GDN_KEY_HEADS_PER_STEP = 2
GDN_PREP_UNROLL = 8
S5_CHUNK = 16
LANES = 128
VMEM_LIMIT = 56 * 1024 * 1024


def _dot(a, b, precision=None):
    return jnp.dot(a, b, preferred_element_type=F32, precision=precision)


def _dot_nt(a, b):
    return lax.dot_general(a, b, (((1,), (1,)), ((), ())), preferred_element_type=F32)


def _dot_tn(a, b):
    return lax.dot_general(a, b, (((0,), (0,)), ((), ())), preferred_element_type=F32)


def _floor_pow2(v, n):
    assert n & (n - 1) == 0
    return jnp.bitwise_and(v, -n)


def _silu(t):
    return t * jax.nn.sigmoid(t)


def _norm_mod(x, nw, shift, scale):
    y = x * lax.rsqrt(jnp.mean(x * x, axis=-1, keepdims=True) + NORM_EPS) * nw
    return y * (1.0 + scale) + shift


def _params(*sem):
    return pltpu.CompilerParams(dimension_semantics=sem, vmem_limit_bytes=VMEM_LIMIT)


def _ada_kernel(c_ref, w_ref, b_ref, o_ref):
    o_ref[0] = _dot(_silu(c_ref[...]), w_ref[0], HIGHEST) + b_ref[0]


def _ada_mod(cvec, ada_w, ada_b):
    depth, d, n = ada_w.shape
    rows = cvec.shape[0]
    tn = n // 4
    return pl.pallas_call(
        _ada_kernel,
        grid=(depth, n // tn),
        in_specs=[pl.BlockSpec((rows, d), lambda i, j: (0, 0)),
                  pl.BlockSpec((1, d, tn), lambda i, j: (i, 0, j)),
                  pl.BlockSpec((1, 1, tn), lambda i, j: (i, 0, j))],
        out_specs=pl.BlockSpec((1, rows, tn), lambda i, j: (i, 0, j)),
        out_shape=jax.ShapeDtypeStruct((depth, rows, n), F32),
        compiler_params=_params("arbitrary", "arbitrary"),
        name="ada_mod",
    )(cvec, ada_w, ada_b.reshape(depth, 1, n))


def _prenorm_kernel(x_ref, nw_ref, mod_ref, o_ref):
    h = _norm_mod(x_ref[...], nw_ref[...], mod_ref[0, 0:1, :], mod_ref[0, 1:2, :])
    o_ref[...] = h.astype(o_ref.dtype)


def _prenorm(x, nw, mod, tile, tiles_per_mod):
    n, d = x.shape
    return pl.pallas_call(
        _prenorm_kernel,
        grid=(n // tile,),
        in_specs=[pl.BlockSpec((tile, d), lambda i: (i, 0)),
                  pl.BlockSpec((1, d), lambda i: (0, 0)),
                  pl.BlockSpec((1, 6, d), lambda i: (i // tiles_per_mod, 0, 0))],
        out_specs=pl.BlockSpec((tile, d), lambda i: (i, 0)),
        out_shape=jax.ShapeDtypeStruct((n, d), BF16),
        compiler_params=_params("parallel"),
        name="prenorm",
    )(x, nw.reshape(1, d), mod)


def _s5_tables(lam_re, lam_im, log_dt, b_re, b_im, c_re, c_im, d_skip, t):
    _, g, p = lam_re.shape
    h = b_re.shape[-1]
    f = lambda a: a.astype(F32)
    lam_re, lam_im, b_re, b_im, c_re, c_im = map(f, (lam_re, lam_im, b_re, b_im, c_re, c_im))
    dt = jnp.exp(f(log_dt))[..., None]
    zr, zi = lam_re * dt, lam_im * dt
    n = jnp.arange(t + 1, dtype=F32)[:, None, None, None]
    mag = jnp.exp(n * zr)
    pr, pi = mag * jnp.cos(n * zi), mag * jnp.sin(n * zi)
    ar, ai = pr[1], pi[1]
    den = lam_re * lam_re + lam_im * lam_im
    fr = ((ar - 1.0) * lam_re + ai * lam_im) / den
    fi = (ai * lam_re - (ar - 1.0) * lam_im) / den
    bbr = fr[..., None] * b_re - fi[..., None] * b_im
    bbi = fr[..., None] * b_im + fi[..., None] * b_re
    car = c_re[None] * pr[:, :, :, None, :] - c_im[None] * pi[:, :, :, None, :]
    cai = c_re[None] * pi[:, :, :, None, :] + c_im[None] * pr[:, :, :, None, :]
    kern = (jnp.einsum('ndgop,dgpi->ndgoi', car, bbr, precision=HIGHEST)
            - jnp.einsum('ndgop,dgpi->ndgoi', cai, bbi, precision=HIGHEST))
    ti = jnp.arange(t)
    lag = ti[None, :] - ti[:, None]
    kf = jnp.where((lag >= 0)[:, :, None, None, None], kern[jnp.clip(lag, 0, t), 0], 0.0)
    kb = jnp.where((lag <= 0)[:, :, None, None, None], kern[jnp.clip(-lag, 0, t), 1], 0.0)
    intra = jnp.transpose(kf + kb, (2, 0, 4, 1, 3)).reshape(g, t * h, t * h)

    def inject(d, powers):
        er, ei = pr[powers, d], pi[powers, d]
        re = er[..., None] * bbr[d][None] - ei[..., None] * bbi[d][None]
        im = er[..., None] * bbi[d][None] + ei[..., None] * bbr[d][None]
        tr = lambda a: jnp.transpose(a, (1, 0, 3, 2)).reshape(g, t * h, p)
        return tr(re), tr(im)

    pad = lambda a: jnp.pad(a, ((0, 0), (0, 0), (0, LANES - p)))
    in_f = inject(0, t - 1 - ti)
    in_b = inject(1, ti)
    w1 = jnp.concatenate([intra] + [pad(a) for a in (*in_f, *in_b)], axis=-1)

    def readout(d, powers):
        tr = lambda a: jnp.transpose(a, (1, 3, 0, 2)).reshape(g, p, t * h)
        return tr(car[powers, d]), tr(-cai[powers, d])

    padr = lambda a: jnp.pad(a, ((0, 0), (0, LANES - p), (0, 0)))
    w2 = jnp.concatenate([padr(a) for a in (*readout(0, ti + 1), *readout(1, t - ti))], axis=1)
    dec = jnp.stack([pr[t, 0], pi[t, 0], pr[t, 1], pi[t, 1]], axis=1)
    dec = jnp.pad(dec, ((0, 0), (0, 4), (0, LANES - p)))
    dsk = jnp.tile(f(d_skip).reshape(g, 1, h), (1, t, 1)).reshape(g, 1, t * h)
    return w1.astype(BF16), w2.astype(BF16), dec, dsk


def _s5_kernel(hc_ref, hl_ref, pg_ref, pt_ref, w1_ref, w2_ref, dec_ref, dsk_ref, yc_ref, yl_ref,
               z_scr, h_scr, u_scr, *, bsz, ncc, ncl, width, row_blk):
    rc, rl = ncc * bsz, ncl * bsz
    t = hl_ref.shape[0]
    cb = max(1, row_blk // bsz)
    w1 = w1_ref[0]
    pg = pg_ref[0]

    @pl.when(pl.program_id(1) == 0)
    def _():
        yc_ref[...] = jnp.zeros(yc_ref.shape, BF16)
        yl_ref[...] = jnp.zeros(yl_ref.shape, BF16)

    def blocks():
        for h_ref, y_ref, base, nch in ((hc_ref, yc_ref, 0, ncc), (hl_ref, yl_ref, rc, ncl)):
            for c0 in range(0, nch, cb):
                n = min(cb, nch - c0)
                yield h_ref, y_ref, c0, n, slice(base + c0 * bsz, base + (c0 + n) * bsz)

    for h_ref, _, c0, n, sl in blocks():
        xcat = jnp.concatenate([h_ref[tt, c0:c0 + n].reshape(n * bsz, LANES) for tt in range(t)], axis=1)
        u = _dot(xcat, pg).astype(BF16)
        u_scr[sl, :] = u
        z_scr[sl, :] = _dot(u, w1)
    dec = dec_ref[0]
    zero = jnp.zeros((bsz, LANES), F32)
    cols = [width + k * LANES for k in range(5)]

    def make_step(are, aim, col, hcol):
        def step(row, carry):
            re, im = carry
            r = pl.multiple_of(row, bsz)
            h_scr[pl.ds(r, bsz), hcol:hcol + LANES] = re
            h_scr[pl.ds(r, bsz), hcol + LANES:hcol + 2 * LANES] = im
            s_re = z_scr[pl.ds(r, bsz), cols[col]:cols[col + 1]]
            s_im = z_scr[pl.ds(r, bsz), cols[col + 1]:cols[col + 2]]
            return are * re - aim * im + s_re, are * im + aim * re + s_im
        return step

    fstep = make_step(dec[0:1, :], dec[1:2, :], 0, 0)
    bstep = make_step(dec[2:3, :], dec[3:4, :], 2, 2 * LANES)
    lax.fori_loop(0, ncc + ncl, lambda c, s: fstep(c * bsz, s), (zero, zero))
    st = lax.fori_loop(0, ncc, lambda i, s: bstep((ncc - 1 - i) * bsz, s), (zero, zero))
    lax.fori_loop(0, ncl, lambda i, s: bstep((ncc + ncl - 1 - i) * bsz, s), st)

    w2 = w2_ref[0]
    dsk = dsk_ref[0]
    pt = pt_ref[0]
    for _, y_ref, c0, n, sl in blocks():
        y = z_scr[sl, 0:width] + _dot(h_scr[sl, :].astype(BF16), w2)
        y = jax.nn.gelu(y + dsk * u_scr[sl, :].astype(F32)).astype(BF16)
        yn = _dot(y, pt)
        for tt in range(t):
            slab = yn[:, tt * LANES:(tt + 1) * LANES].astype(BF16).reshape(n, bsz, LANES)
            y_ref[tt, c0:c0 + n] = y_ref[tt, c0:c0 + n] + slab


def _s5_lane_perm(t, hch):
    ng = LANES // hch
    pg = np.zeros((ng, t, LANES, t, hch), np.float32)
    for g in range(ng):
        for tt in range(t):
            pg[g, tt, g * hch + np.arange(hch), tt, np.arange(hch)] = 1.0
    return pg.reshape(ng, t * LANES, t * hch)


def _s5_core(hc, hl, tables, row_blk=512):
    w1, w2, dec, dsk = tables
    t, ncc, bsz, d = hc.shape
    ncl = hl.shape[1]
    width = w2.shape[2]
    hch = width // t
    ng = LANES // hch
    rows = (ncc + ncl) * bsz
    pg = _s5_lane_perm(t, hch)
    kern = functools.partial(_s5_kernel, bsz=bsz, ncc=ncc, ncl=ncl, width=width, row_blk=row_blk)
    gmap = lambda j, q: (j * ng + q, 0, 0)
    qmap = lambda j, q: (q, 0, 0)
    once = pl.Buffered(1)
    return pl.pallas_call(
        kern,
        grid=(d // LANES, ng),
        in_specs=[pl.BlockSpec((t, ncc, bsz, LANES), lambda j, q: (0, 0, 0, j), pipeline_mode=once),
                  pl.BlockSpec((t, ncl, bsz, LANES), lambda j, q: (0, 0, 0, j), pipeline_mode=once),
                  pl.BlockSpec((1,) + pg.shape[1:], qmap), pl.BlockSpec((1, t * hch, t * LANES), qmap),
                  pl.BlockSpec((1,) + w1.shape[1:], gmap), pl.BlockSpec((1,) + w2.shape[1:], gmap),
                  pl.BlockSpec((1,) + dec.shape[1:], gmap), pl.BlockSpec((1,) + dsk.shape[1:], gmap)],
        out_specs=[pl.BlockSpec((t, ncc, bsz, LANES), lambda j, q: (0, 0, 0, j), pipeline_mode=once),
                   pl.BlockSpec((t, ncl, bsz, LANES), lambda j, q: (0, 0, 0, j), pipeline_mode=once)],
        out_shape=[jax.ShapeDtypeStruct(hc.shape, BF16), jax.ShapeDtypeStruct(hl.shape, BF16)],
        scratch_shapes=[pltpu.VMEM((rows, w1.shape[2]), F32), pltpu.VMEM((rows, 4 * LANES), F32),
                        pltpu.VMEM((rows, width), BF16)],
        compiler_params=_params("parallel", "arbitrary"),
        name="s5_core",
    )(hc, hl, jnp.asarray(pg, dtype=BF16), jnp.asarray(pg.transpose(0, 2, 1), dtype=BF16), w1, w2, dec, dsk)


def _time_major(hm, bsz, t):
    n, d = hm.shape
    return jnp.transpose(hm.reshape(bsz, n // bsz // t, t, d), (2, 1, 0, 3))


def _token_major(y):
    t, nc, bsz, d = y.shape
    return jnp.transpose(y, (2, 1, 0, 3)).reshape(bsz * nc * t, d)


def _mixout_kernel(y_ref, w_ref, x_ref, mod_ref, o_ref, *, glu):
    r = _dot(y_ref[...], w_ref[...])
    if glu:
        half = r.shape[-1] // 2
        r = r[:, :half] * jax.nn.sigmoid(r[:, half:])
    o_ref[...] = x_ref[...] + mod_ref[0, 2:3, :] * r


def _mixout(y, w, x, mod, tile, tiles_per_mod, glu):
    n, d = x.shape
    k, nn = w.shape
    return pl.pallas_call(
        functools.partial(_mixout_kernel, glu=glu),
        grid=(n // tile,),
        in_specs=[pl.BlockSpec((tile, k), lambda i: (i, 0)),
                  pl.BlockSpec((k, nn), lambda i: (0, 0)),
                  pl.BlockSpec((tile, d), lambda i: (i, 0)),
                  pl.BlockSpec((1, 6, d), lambda i: (i // tiles_per_mod, 0, 0))],
        out_specs=pl.BlockSpec((tile, d), lambda i: (i, 0)),
        out_shape=jax.ShapeDtypeStruct((n, d), F32),
        compiler_params=_params("parallel"),
        name="mixer_out",
    )(y, w, x, mod)


def _mlp_kernel(x_ref, nw_ref, mod_ref, w1_ref, w2_ref, fw_ref, o_ref, *, ff_blk, final):
    x = x_ref[...]
    h = _norm_mod(x, nw_ref[...], mod_ref[0, 3:4, :], mod_ref[0, 4:5, :]).astype(BF16)
    acc = jnp.zeros(x.shape, F32)
    for f0 in range(0, w1_ref.shape[1], ff_blk):
        a = jnp.maximum(_dot(h, w1_ref[:, f0:f0 + ff_blk]), 0.0)
        acc = acc + _dot((a * a).astype(BF16), w2_ref[f0:f0 + ff_blk, :])
    y = x + mod_ref[0, 5:6, :] * acc
    if final:
        y = y * lax.rsqrt(jnp.mean(y * y, axis=-1, keepdims=True) + NORM_EPS) * fw_ref[...]
    o_ref[...] = y


def _mlp(x, nw, mod, w1, w2, fw, tile, tiles_per_mod, final):
    n, d = x.shape
    ff = w1.shape[1]
    return pl.pallas_call(
        functools.partial(_mlp_kernel, ff_blk=min(1024, ff), final=final),
        grid=(n // tile,),
        in_specs=[pl.BlockSpec((tile, d), lambda i: (i, 0)),
                  pl.BlockSpec((1, d), lambda i: (0, 0)),
                  pl.BlockSpec((1, 6, d), lambda i: (i // tiles_per_mod, 0, 0)),
                  pl.BlockSpec((d, ff), lambda i: (0, 0)),
                  pl.BlockSpec((ff, d), lambda i: (0, 0)),
                  pl.BlockSpec((1, d), lambda i: (0, 0))],
        out_specs=pl.BlockSpec((tile, d), lambda i: (i, 0)),
        out_shape=jax.ShapeDtypeStruct((n, d), F32),
        compiler_params=_params("parallel"),
        name="mlp",
    )(x, nw.reshape(1, d), mod, w1, w2, fw.reshape(1, d))


def _gdn_in_kernel(x_ref, nw_ref, mod_ref, w_ref, wg_ref, cw_ref, gp_ref, p_ref, g_ref, gt_ref, acc_scr,
                   *, period, col0, qk, vd, tn, hv, hdim):
    tb = x_ref.shape[0]
    h = _norm_mod(x_ref[...], nw_ref[...], mod_ref[0, 0:1, :], mod_ref[0, 1:2, :])
    hb = h.astype(BF16)
    gl = _dot(h, wg_ref[...], HIGHEST)
    lane = lax.broadcasted_iota(jnp.int32, (1, LANES), 1)
    beta = jax.nn.sigmoid(gl)
    t = gl + gp_ref[1:2, :]
    g = gp_ref[0:1, :] * (jnp.maximum(t, 0.0) + jnp.log1p(jnp.exp(-jnp.abs(t))))
    r = lax.broadcasted_iota(jnp.int32, (tb, tb), 0)
    c = lax.broadcasted_iota(jnp.int32, (tb, tb), 1)
    same = _floor_pow2(r, GDN_CHUNK) == _floor_pow2(c, GDN_CHUNK)
    gcf = _dot(jnp.where(same & (c <= r), 1.0, 0.0), g, HIGHEST)
    gcb = _dot(jnp.where(same & (c >= r), 1.0, 0.0), g, HIGHEST)
    out = jnp.where(lane < 2 * hv, beta, jnp.where(lane < 3 * hv, gcf, jnp.where(lane < 4 * hv, gcb, 0.0)))
    g_ref[...] = out
    for cc in range(tb // GDN_CHUNK):
        gt_ref[cc] = out[cc * GDN_CHUNK:(cc + 1) * GDN_CHUNK, :].T

    assert tb % period == 0
    row = lax.broadcasted_iota(jnp.int32, (period, 1), 0)
    edge = [(tap, ((row + tap - 2 >= 0) & (row + tap - 2 < period)).astype(F32)) for tap in (0, 1, 3, 4)]

    for bi, o0 in enumerate(range(0, p_ref.shape[1], tn)):
        c0 = col0 + o0
        acc = _dot(hb, w_ref[:, c0:c0 + tn])
        if qk <= c0 < qk + vd:
            p_ref[:, o0:o0 + tn] = acc.astype(BF16)
            continue
        acc_ref = acc_scr.at[bi % 2]
        acc_ref[...] = acc
        scale = hdim ** -0.5 if c0 < qk else (1.0 if c0 < 2 * qk + vd else None)
        for h0 in range(0, tn, hdim):
            cw = cw_ref[:, c0 + h0:c0 + h0 + hdim]
            taps = [(tap, ok * cw[tap:tap + 1, :]) for tap, ok in edge]
            for r0 in range(0, tb, period):
                a = acc_ref[r0:r0 + period, h0:h0 + hdim]
                y = a * cw[2:3, :]
                for tap, wt in taps:
                    y = y + pltpu.roll(a, (period - (tap - 2)) % period, 0) * wt
                y = _silu(y)
                if scale is not None:
                    y = y * (lax.rsqrt(jnp.sum(y * y, axis=-1, keepdims=True) + NORM_EPS) * scale)
                p_ref[r0:r0 + period, o0 + h0:o0 + h0 + hdim] = y.astype(BF16)


def _gdn_in(x, nw, mod, w_main, w_gate, cw, gp, tile, tiles_per_mod, period, col0, tn, qk, vd, hv, hdim):
    n, d = x.shape
    n_main = w_main.shape[1]
    kern = functools.partial(_gdn_in_kernel, period=period, col0=col0, qk=qk, vd=vd, tn=tn, hv=hv, hdim=hdim)
    return pl.pallas_call(
        kern,
        grid=(n // tile,),
        in_specs=[pl.BlockSpec((tile, d), lambda i: (i, 0)),
                  pl.BlockSpec((1, d), lambda i: (0, 0)),
                  pl.BlockSpec((1, 6, d), lambda i: (i // tiles_per_mod, 0, 0)),
                  pl.BlockSpec((d, n_main), lambda i: (0, 0)),
                  pl.BlockSpec((d, LANES), lambda i: (0, 0)),
                  pl.BlockSpec((8, n_main), lambda i: (0, 0)),
                  pl.BlockSpec((2, LANES), lambda i: (0, 0))],
        out_specs=[pl.BlockSpec((tile, n_main - col0), lambda i: (i, 0)),
                   pl.BlockSpec((tile, LANES), lambda i: (i, 0)),
                   pl.BlockSpec((tile // GDN_CHUNK, LANES, GDN_CHUNK), lambda i: (i, 0, 0))],
        out_shape=[jax.ShapeDtypeStruct((n, n_main - col0), BF16),
                   jax.ShapeDtypeStruct((n, LANES), F32),
                   jax.ShapeDtypeStruct((n // GDN_CHUNK, LANES, GDN_CHUNK), F32)],
        scratch_shapes=[pltpu.VMEM((2, tile, tn), F32)],
        compiler_params=_params("parallel"),
        name="gdn_in",
    )(x, nw.reshape(1, d), mod, w_main, w_gate, cw, gp)


def _gdn_masks(ch, rep):
    nco = 2 * rep
    r, c = np.meshgrid(np.arange(ch), np.arange(ch), indexing="ij")
    per_dir = []
    for lower in (True, False):
        m = [(c <= r) if lower else (c >= r), (c < r) if lower else (c > r)]
        b = 1
        while b < ch:
            same = (r // (2 * b)) == (c // (2 * b))
            hi_r, hi_c = (r % (2 * b)) >= b, (c % (2 * b)) >= b
            m.append(same & ((hi_r & ~hi_c) if lower else (~hi_r & hi_c)))
            b *= 2
        per_dir.append(np.stack(m))
    cm = np.concatenate([per_dir[j // rep] for j in range(nco)], axis=2).astype(np.float32)
    blk = np.arange(nco * ch) // ch
    return cm, (blk[:, None] == blk[None, :]).astype(np.float32)


def _gdn_kernel(q_ref, k_ref, v_ref, z_ref, kc_ref, vc_ref, g_ref, gt_ref, gc_ref, gct_ref, ow_ref, cm_ref, lv_ref, bd_ref,
                o_ref, of_s, ob_s, st_s, uw_s, in_s, kt_s, *, hv, khb, rep, hdim, ncc, ncl, unroll):
    ch = GDN_CHUNK
    nco = 2 * rep
    nlvl = lv_ref.shape[0]
    kh0 = pl.program_id(1) * khb
    lane = lax.broadcasted_iota(jnp.int32, (1, LANES), 1)
    lcat = lax.broadcasted_iota(jnp.int32, (1, nco * ch), 1)

    def lanes_of(khl, j):
        head = (kh0 + khl) * rep + j % rep
        return (j // rep) * hv + head, (2 + j // rep) * hv + head

    def column(gt, idx):
        return jnp.sum(jnp.where(lane == idx, gt, 0.0), axis=1, keepdims=True)

    def spread(cols):
        out = jnp.broadcast_to(cols[nco - 1], (ch, nco * ch))
        for j in range(nco - 2, -1, -1):
            out = jnp.where(lcat < (j + 1) * ch, cols[j], out)
        return out

    def block_diag(xb):
        return jnp.concatenate([xb] * nco, axis=0) * bd_ref[...]

    def prepare(insts):
        st = []
        for khl, k_r, q_r, v_r, g_r, gt_r, c, slot in insts:
            rows = pl.ds(pl.multiple_of(c * ch, ch), ch)
            kb = k_r[rows, khl * hdim:(khl + 1) * hdim]
            kcat = jnp.concatenate([kb] * nco, axis=0)
            if q_r is None:
                kkc, qkc = _dot_nt(kb, kcat), None
            else:
                prod = _dot_nt(jnp.concatenate([kb, q_r[rows, khl * hdim:(khl + 1) * hdim]], axis=0), kcat)
                kkc, qkc = prod[:ch], prod[ch:]
            gtile = g_r[rows, :]
            betas, gcols, grows = [], [], []
            for j in range(nco):
                bl, gl = lanes_of(khl, j)
                betas.append(column(gtile, bl))
                gcols.append(column(gtile, gl))
                grows.append(gt_r[c, pl.ds(gl, 1), :])
            dlog = spread(gcols) - jnp.concatenate(grows, axis=1)
            decay = jnp.exp(jnp.where(cm_ref[0] > 0.0, dlog, -jnp.inf))
            ac = spread(betas) * kkc * decay * cm_ref[1]
            xb = ((cm_ref[0] - cm_ref[1]) - ac * cm_ref[2]).astype(BF16)
            st.append([rows, kb, qkc, betas, gcols, decay, ac.astype(BF16), xb])
        for lvl in range(1, nlvl):
            ps = [_dot(s[6] * lv_ref[lvl], block_diag(s[7])) for s in st]
            for s, p in zip(st, ps):
                s[7] = s[7] - _dot(s[7], block_diag(p.astype(BF16))).astype(BF16)
        outs = []
        for (khl, k_r, q_r, v_r, g_r, gt_r, c, slot), (rows, kb, qkc, betas, gcols, decay, ac, xc) in zip(insts, st):
            kf = kb.astype(F32)
            rst = []
            for j in range(nco):
                v0 = (khl * rep + j % rep) * hdim
                vf = v_r[rows, v0:v0 + hdim].astype(F32)
                rst.append(jnp.concatenate([vf * betas[j], kf * (betas[j] * jnp.exp(gcols[j]))], axis=1))
            uw = _dot(block_diag(xc), jnp.concatenate(rst, axis=0).astype(BF16))
            intra = None if qkc is None else (qkc * decay * cm_ref[0]).astype(BF16)
            outs.append((uw, kf.T.astype(BF16), intra))
        for (khl, k_r, q_r, v_r, g_r, gt_r, c, slot), (uw, kt, intra) in zip(insts, outs):
            for j in range(nco):
                uw_s[slot, khl * nco + j] = uw[j * ch:(j + 1) * ch, :].astype(BF16)
                if intra is not None:
                    in_s[c, khl * nco + j] = intra[:, j * ch:(j + 1) * ch]
            kt_s[slot, khl] = kt

    def advance(items):
        mid = []
        for khl, j, q_r, g_r, gt_r, c, slot in items:
            rows = pl.ds(pl.multiple_of(c * ch, ch), ch)
            _, gl = lanes_of(khl, j)
            gcol = column(g_r[rows, :], gl)
            grow = gt_r[c, pl.ds(gl, 1), :]
            glast = grow[:, ch - 1:ch] if j < rep else grow[:, 0:1]
            sidx = khl * nco + j
            s = st_s[sidx]
            lhs = uw_s[slot, sidx, :, hdim:]
            if q_r is not None:
                qg = q_r[rows, khl * hdim:(khl + 1) * hdim].astype(F32) * jnp.exp(gcol)
                lhs = jnp.concatenate([lhs, qg.astype(BF16)], axis=0)
            mid.append((rows, gcol, glast, sidx, s, _dot(lhs, s.astype(BF16))))
        res = []
        for (khl, j, q_r, g_r, gt_r, c, slot), (rows, gcol, glast, sidx, s, ws) in zip(items, mid):
            vnew = uw_s[slot, sidx, :, :hdim].astype(F32) - ws[:ch]
            snew = s * jnp.exp(glast) + _dot(kt_s[slot, khl], (vnew * jnp.exp(glast - gcol)).astype(BF16))
            o = None if q_r is None else ws[ch:] + _dot(in_s[c, sidx], vnew.astype(BF16))
            res.append((snew, o))
        for (khl, j, q_r, g_r, gt_r, c, slot), (rows, gcol, glast, sidx, s, ws), (snew, o) in zip(items, mid, res):
            st_s[sidx] = snew
            if o is not None:
                v0 = (khl * rep + j % rep) * hdim
                (of_s if j < rep else ob_s)[rows, v0:v0 + hdim] = o.astype(BF16)

    st_s[...] = jnp.zeros(st_s.shape, F32)
    uc = min(unroll, ncc)

    def prep_ctx(i, carry):
        prepare([(khl, kc_ref, None, vc_ref, gc_ref, gct_ref, i * uc + t, i * uc + t)
                 for t in range(uc) for khl in range(khb)])
        return carry

    def prep_lat(i, carry):
        prepare([(khl, k_ref, q_ref, v_ref, g_ref, gt_ref, i * unroll + t, ncc + i * unroll + t)
                 for t in range(unroll) for khl in range(khb)])
        return carry

    def adv_ctx(i, carry):
        advance([(khl, j, None, gc_ref, gct_ref, i if j < rep else ncc - 1 - i, i if j < rep else ncc - 1 - i)
                 for khl in range(khb) for j in range(nco)])
        return carry

    def adv_lat(i, carry):
        advance([(khl, j, q_ref, g_ref, gt_ref, i if j < rep else ncl - 1 - i,
                  ncc + (i if j < rep else ncl - 1 - i)) for khl in range(khb) for j in range(nco)])
        return carry

    lax.fori_loop(0, ncc // uc, prep_ctx, 0)
    lax.fori_loop(0, ncl // unroll, prep_lat, 0)
    lax.fori_loop(0, ncc, adv_ctx, 0)
    lax.fori_loop(0, ncl, adv_lat, 0)

    blk = min(256, ncl * ch)
    ow = ow_ref[...]
    for r0 in range(0, ncl * ch, blk):
        for v0 in range(0, khb * rep * hdim, hdim):
            o = of_s[r0:r0 + blk, v0:v0 + hdim].astype(F32) + ob_s[r0:r0 + blk, v0:v0 + hdim].astype(F32)
            o = o * lax.rsqrt(jnp.mean(o * o, axis=-1, keepdims=True) + NORM_EPS) * ow
            o_ref[r0:r0 + blk, v0:v0 + hdim] = (o * _silu(z_ref[r0:r0 + blk, v0:v0 + hdim].astype(F32))).astype(BF16)


def _gdn_core(p_lat, p_ctx, g_lat, gt_lat, g_ctx, gt_ctx, onorm_w, bsz, hk, hv, hdim, khb=GDN_KEY_HEADS_PER_STEP):
    l = p_lat.shape[0] // bsz
    lc = p_ctx.shape[0] // bsz
    ch = GDN_CHUNK
    ncc, ncl = lc // ch, l // ch
    rep = hv // hk
    nco = 2 * rep
    kw, vw = khb * hdim, khb * rep * hdim
    qk, vd = hk * hdim, hv * hdim
    unroll = min(GDN_PREP_UNROLL, ncl)
    assert hk % khb == 0 and ncl % unroll == 0 and ncc % min(unroll, ncc) == 0
    assert qk % vw == 0 and vd % vw == 0
    kern = functools.partial(_gdn_kernel, hv=hv, khb=khb, rep=rep, hdim=hdim, ncc=ncc, ncl=ncl, unroll=unroll)
    kcol, vcol, zcol = (qk + vd) // kw, (2 * qk + vd) // vw, qk // vw
    cm, bd = _gdn_masks(ch, rep)
    return pl.pallas_call(
        kern,
        grid=(bsz, hk // khb),
        in_specs=[pl.BlockSpec((l, kw), lambda b, h: (b, h)),
                  pl.BlockSpec((l, kw), lambda b, h: (b, kcol + h)),
                  pl.BlockSpec((l, vw), lambda b, h: (b, vcol + h)),
                  pl.BlockSpec((l, vw), lambda b, h: (b, zcol + h)),
                  pl.BlockSpec((lc, kw), lambda b, h: (b, h)),
                  pl.BlockSpec((lc, vw), lambda b, h: (b, qk // vw + h)),
                  pl.BlockSpec((l, LANES), lambda b, h: (b, 0)),
                  pl.BlockSpec((ncl, LANES, ch), lambda b, h: (b, 0, 0)),
                  pl.BlockSpec((lc, LANES), lambda b, h: (b, 0)),
                  pl.BlockSpec((ncc, LANES, ch), lambda b, h: (b, 0, 0)),
                  pl.BlockSpec((1, hdim), lambda b, h: (0, 0)),
                  pl.BlockSpec(cm[:3].shape, lambda b, h: (0, 0, 0)),
                  pl.BlockSpec(cm[2:].shape, lambda b, h: (0, 0, 0)),
                  pl.BlockSpec(bd.shape, lambda b, h: (0, 0))],
        out_specs=pl.BlockSpec((l, vw), lambda b, h: (b, h)),
        out_shape=jax.ShapeDtypeStruct((bsz * l, vd), BF16),
        scratch_shapes=[pltpu.VMEM((l, vw), BF16), pltpu.VMEM((l, vw), BF16),
                        pltpu.VMEM((khb * nco, hdim, hdim), F32),
                        pltpu.VMEM((ncc + ncl, khb * nco, ch, 2 * hdim), BF16),
                        pltpu.VMEM((ncl, khb * nco, ch, ch), BF16),
                        pltpu.VMEM((ncc + ncl, khb, hdim, ch), BF16)],
        compiler_params=_params("parallel", "arbitrary"),
        name="gdn_core",
    )(p_lat, p_lat, p_lat, p_lat, p_ctx, p_ctx, g_lat, gt_lat, g_ctx, gt_ctx, onorm_w.reshape(1, hdim),
      jnp.asarray(cm[:3]), jnp.asarray(cm[2:], dtype=BF16), jnp.asarray(bd, dtype=BF16))


def kernel(x, c, ctx, c_ctx, ada_w, ada_b, norm1_w, norm2_w, mlp_w1, mlp_w2, s5_lam_re, s5_lam_im, s5_log_dt, s5_b_re, s5_b_im, s5_c_re, s5_c_im, s5_d, s5_w_glu, gdn_w_in, gdn_conv_w, gdn_a_log, gdn_dt_bias, gdn_onorm_w, gdn_w_out, final_norm_w):
    bsz, l, d = x.shape
    lc = ctx.shape[1]
    depth = ada_w.shape[0]
    assert depth == 2 and s5_lam_re.shape[0] == 1 and gdn_w_in.shape[0] == 1
    groups = s5_lam_re.shape[2]
    hv = gdn_a_log.shape[2]
    hdim = gdn_onorm_w.shape[1]
    vd = gdn_w_out.shape[1]
    qk = (gdn_conv_w.shape[2] - vd) // 2
    hk = qk // hdim
    assert l % GRID_W == 0 and l % GDN_CHUNK == 0 and lc % GDN_CHUNK == 0
    assert l % S5_CHUNK == 0 and lc % S5_CHUNK == 0 and 4 * hv <= LANES

    tb_l, tb_c = min(256, l), lc
    tm_l, tm_c = min(512, l), min(512, bsz * lc)
    xl = x.reshape(bsz * l, d)
    xc = ctx.reshape(bsz * lc, d)

    rows = -(-(bsz + 1) // 8) * 8
    cvec = jnp.zeros((rows, d), F32).at[:bsz].set(c).at[bsz].set(c_ctx)
    mod = _ada_mod(cvec, ada_w, ada_b).reshape(depth, rows, 6, d)
    big = 1 << 30

    mod_l, mod_c = mod[0, :bsz], mod[0, bsz:bsz + 1]
    hl = _prenorm(xl, norm1_w[0], mod_l, tb_l, l // tb_l)
    hc = _prenorm(xc, norm1_w[0], mod_c, tb_c, big)
    tables = _s5_tables(s5_lam_re[0], s5_lam_im[0], s5_log_dt[0], s5_b_re[0], s5_b_im[0],
                        s5_c_re[0], s5_c_im[0], s5_d[0], S5_CHUNK)
    yc, yl = _s5_core(_time_major(hc, bsz, S5_CHUNK), _time_major(hl, bsz, S5_CHUNK), tables)
    w_glu = s5_w_glu[0].astype(BF16)
    w1, w2 = mlp_w1[0].astype(BF16), mlp_w2[0].astype(BF16)
    xl = _mixout(_token_major(yl), w_glu, xl, mod_l, tm_l, l // tm_l, True)
    xc = _mixout(_token_major(yc), w_glu, xc, mod_c, tm_c, big, True)
    xl = _mlp(xl, norm2_w[0], mod_l, w1, w2, final_norm_w, tm_l, l // tm_l, False)
    xc = _mlp(xc, norm2_w[0], mod_c, w1, w2, final_norm_w, tm_c, big, False)

    mod_l, mod_c = mod[1, :bsz], mod[1, bsz:bsz + 1]
    w_in = gdn_w_in[0]
    n_main = 2 * qk + 2 * vd
    w_main = w_in[:, :n_main].astype(BF16)
    w_gate = jnp.pad(w_in[:, n_main:], ((0, 0), (0, LANES - 4 * hv)))
    conv_w = gdn_conv_w[0].astype(F32)
    cw = jnp.concatenate([conv_w[:, :qk], jnp.zeros((conv_w.shape[0], vd), F32), conv_w[:, qk:]], axis=1)
    cw = jnp.pad(cw, ((0, 8 - cw.shape[0]), (0, 0)))
    neg_a = jnp.pad(-jnp.exp(gdn_a_log[0].astype(F32)).reshape(-1), (2 * hv, LANES - 4 * hv))
    dtb = jnp.pad(gdn_dt_bias[0].astype(F32).reshape(-1), (2 * hv, LANES - 4 * hv))
    gp = jnp.stack([neg_a, dtb])
    tn = min(512, qk)
    p_l, g_l, gt_l = _gdn_in(xl, norm1_w[1], mod_l, w_main, w_gate, cw, gp, tb_l, l // tb_l, GRID_W,
                             0, tn, qk, vd, hv, hdim)
    p_c, g_c, gt_c = _gdn_in(xc, norm1_w[1], mod_c, w_main, w_gate, cw, gp, tb_c, big, lc,
                             qk + vd, tn, qk, vd, hv, hdim)
    gated = _gdn_core(p_l, p_c, g_l, gt_l, g_c, gt_c, gdn_onorm_w[0].astype(F32), bsz, hk, hv, hdim)
    xl = _mixout(gated, gdn_w_out[0].astype(BF16), xl, mod_l, tm_l, l // tm_l, False)
    out = _mlp(xl, norm2_w[1], mod_l, mlp_w1[1].astype(BF16), mlp_w2[1].astype(BF16), final_norm_w,
               tm_l, l // tm_l, True)
    return out.reshape(bsz, l, d)
```

```python
import functools
import math

import jax
import jax.numpy as jnp
import numpy as np
from jax import lax
from jax.experimental import pallas as pl
from jax.experimental.pallas import tpu as pltpu

F32 = jnp.float32
BF16 = jnp.bfloat16
HIGHEST = lax.Precision.HIGHEST

NORM_EPS = 1e-6
GRID_W = 64
GDN_CHUNK = 64
GDN_KEY_HEADS_PER_STEP = 2
GDN_PREP_UNROLL = 8
S5_CHUNK = 16
LANES = 128
VMEM_LIMIT = 56 * 1024 * 1024


def _dot(a, b, precision=None):
    return jnp.dot(a, b, preferred_element_type=F32, precision=precision)


def _dot_nt(a, b):
    return lax.dot_general(a, b, (((1,), (1,)), ((), ())), preferred_element_type=F32)


def _dot_tn(a, b):
    return lax.dot_general(a, b, (((0,), (0,)), ((), ())), preferred_element_type=F32)


def _floor_pow2(v, n):
    assert n & (n - 1) == 0
    return jnp.bitwise_and(v, -n)


def _silu(t):
    return t * jax.nn.sigmoid(t)


def _norm_mod(x, nw, shift, scale):
    y = x * lax.rsqrt(jnp.mean(x * x, axis=-1, keepdims=True) + NORM_EPS) * nw
    return y * (1.0 + scale) + shift


def _params(*sem):
    return pltpu.CompilerParams(dimension_semantics=sem, vmem_limit_bytes=VMEM_LIMIT)


def _ada_kernel(c_ref, w_ref, b_ref, o_ref):
    o_ref[0] = _dot(_silu(c_ref[...]), w_ref[0], HIGHEST) + b_ref[0]


def _ada_mod(cvec, ada_w, ada_b):
    depth, d, n = ada_w.shape
    rows = cvec.shape[0]
    tn = n // 4
    return pl.pallas_call(
        _ada_kernel,
        grid=(depth, n // tn),
        in_specs=[pl.BlockSpec((rows, d), lambda i, j: (0, 0)),
                  pl.BlockSpec((1, d, tn), lambda i, j: (i, 0, j)),
                  pl.BlockSpec((1, 1, tn), lambda i, j: (i, 0, j))],
        out_specs=pl.BlockSpec((1, rows, tn), lambda i, j: (i, 0, j)),
        out_shape=jax.ShapeDtypeStruct((depth, rows, n), F32),
        compiler_params=_params("arbitrary", "arbitrary"),
        name="ada_mod",
    )(cvec, ada_w, ada_b.reshape(depth, 1, n))


def _prenorm_kernel(x_ref, nw_ref, mod_ref, o_ref):
    h = _norm_mod(x_ref[...], nw_ref[...], mod_ref[0, 0:1, :], mod_ref[0, 1:2, :])
    o_ref[...] = h.astype(o_ref.dtype)


def _prenorm(x, nw, mod, tile, tiles_per_mod):
    n, d = x.shape
    return pl.pallas_call(
        _prenorm_kernel,
        grid=(n // tile,),
        in_specs=[pl.BlockSpec((tile, d), lambda i: (i, 0)),
                  pl.BlockSpec((1, d), lambda i: (0, 0)),
                  pl.BlockSpec((1, 6, d), lambda i: (i // tiles_per_mod, 0, 0))],
        out_specs=pl.BlockSpec((tile, d), lambda i: (i, 0)),
        out_shape=jax.ShapeDtypeStruct((n, d), BF16),
        compiler_params=_params("parallel"),
        name="prenorm",
    )(x, nw.reshape(1, d), mod)


def _s5_tables(lam_re, lam_im, log_dt, b_re, b_im, c_re, c_im, d_skip, t):
    _, g, p = lam_re.shape
    h = b_re.shape[-1]
    f = lambda a: a.astype(F32)
    lam_re, lam_im, b_re, b_im, c_re, c_im = map(f, (lam_re, lam_im, b_re, b_im, c_re, c_im))
    dt = jnp.exp(f(log_dt))[..., None]
    zr, zi = lam_re * dt, lam_im * dt
    n = jnp.arange(t + 1, dtype=F32)[:, None, None, None]
    mag = jnp.exp(n * zr)
    pr, pi = mag * jnp.cos(n * zi), mag * jnp.sin(n * zi)
    ar, ai = pr[1], pi[1]
    den = lam_re * lam_re + lam_im * lam_im
    fr = ((ar - 1.0) * lam_re + ai * lam_im) / den
    fi = (ai * lam_re - (ar - 1.0) * lam_im) / den
    bbr = fr[..., None] * b_re - fi[..., None] * b_im
    bbi = fr[..., None] * b_im + fi[..., None] * b_re
    car = c_re[None] * pr[:, :, :, None, :] - c_im[None] * pi[:, :, :, None, :]
    cai = c_re[None] * pi[:, :, :, None, :] + c_im[None] * pr[:, :, :, None, :]
    kern = jnp.einsum('ndgoq,dgqi->ndgoi', jnp.concatenate([car, -cai], axis=-1),
                      jnp.concatenate([bbr, bbi], axis=2), precision=HIGHEST)
    ti = jnp.arange(t)
    lag = np.arange(t)[None, :] - np.arange(t)[:, None]
    place = np.stack([(lag == n) for n in range(t + 1)] + [(-lag == n) for n in range(t + 1)])
    place = place.reshape(2, t + 1, t, t).astype(np.float32)
    intra = jnp.einsum('dnst,ndgoi->gsito', place, kern, precision=HIGHEST).reshape(g, t * h, t * h)

    def inject(d, powers):
        er, ei = pr[powers, d], pi[powers, d]
        re = er[..., None] * bbr[d][None] - ei[..., None] * bbi[d][None]
        im = er[..., None] * bbi[d][None] + ei[..., None] * bbr[d][None]
        tr = lambda a: jnp.transpose(a, (1, 0, 3, 2)).reshape(g, t * h, p)
        return tr(re), tr(im)

    pad = lambda a: jnp.pad(a, ((0, 0), (0, 0), (0, LANES - p)))
    in_f = inject(0, t - 1 - ti)
    in_b = inject(1, ti)
    w1 = jnp.concatenate([intra] + [pad(a) for a in (*in_f, *in_b)], axis=-1)

    def readout(d, powers):
        tr = lambda a: jnp.transpose(a, (1, 3, 0, 2)).reshape(g, p, t * h)
        return tr(car[powers, d]), tr(-cai[powers, d])

    padr = lambda a: jnp.pad(a, ((0, 0), (0, LANES - p), (0, 0)))
    w2 = jnp.concatenate([padr(a) for a in (*readout(0, ti + 1), *readout(1, t - ti))], axis=1)
    dec = jnp.stack([pr[t, 0], pi[t, 0], pr[t, 1], pi[t, 1]], axis=1)
    dec = jnp.pad(dec, ((0, 0), (0, 4), (0, LANES - p)))
    dsk = jnp.tile(f(d_skip).reshape(g, 1, h), (1, t, 1)).reshape(g, 1, t * h)
    return w1.astype(BF16), w2.astype(BF16), dec, dsk


def _s5_kernel(hc_ref, hl_ref, pg_ref, pt_ref, w1_ref, w2_ref, dec_ref, dsk_ref, yc_ref, yl_ref,
               z_scr, h_scr, u_scr, *, bsz, ncc, ncl, width, row_blk):
    rc, rl = ncc * bsz, ncl * bsz
    t = hl_ref.shape[0]
    cb = max(1, row_blk // bsz)
    w1 = w1_ref[0]
    pg = pg_ref[0]

    @pl.when(pl.program_id(1) == 0)
    def _():
        yc_ref[...] = jnp.zeros(yc_ref.shape, BF16)
        yl_ref[...] = jnp.zeros(yl_ref.shape, BF16)

    def blocks():
        for h_ref, y_ref, base, nch in ((hc_ref, yc_ref, 0, ncc), (hl_ref, yl_ref, rc, ncl)):
            for c0 in range(0, nch, cb):
                n = min(cb, nch - c0)
                yield h_ref, y_ref, c0, n, slice(base + c0 * bsz, base + (c0 + n) * bsz)

    for h_ref, _, c0, n, sl in blocks():
        xcat = jnp.concatenate([h_ref[tt, c0:c0 + n].reshape(n * bsz, LANES) for tt in range(t)], axis=1)
        u = _dot(xcat, pg).astype(BF16)
        u_scr[sl, :] = u
        z_scr[sl, :] = _dot(u, w1)
    dec = dec_ref[0]
    zero = jnp.zeros((bsz, LANES), F32)
    cols = [width + k * LANES for k in range(5)]

    def make_step(are, aim, col, hcol):
        def step(row, carry):
            re, im = carry
            r = pl.multiple_of(row, bsz)
            h_scr[pl.ds(r, bsz), hcol:hcol + LANES] = re
            h_scr[pl.ds(r, bsz), hcol + LANES:hcol + 2 * LANES] = im
            s_re = z_scr[pl.ds(r, bsz), cols[col]:cols[col + 1]]
            s_im = z_scr[pl.ds(r, bsz), cols[col + 1]:cols[col + 2]]
            return are * re - aim * im + s_re, are * im + aim * re + s_im
        return step

    fstep = make_step(dec[0:1, :], dec[1:2, :], 0, 0)
    bstep = make_step(dec[2:3, :], dec[3:4, :], 2, 2 * LANES)
    lax.fori_loop(0, ncc + ncl, lambda c, s: fstep(c * bsz, s), (zero, zero))
    st = lax.fori_loop(0, ncc, lambda i, s: bstep((ncc - 1 - i) * bsz, s), (zero, zero))
    lax.fori_loop(0, ncl, lambda i, s: bstep((ncc + ncl - 1 - i) * bsz, s), st)

    w2 = w2_ref[0]
    dsk = dsk_ref[0]
    pt = pt_ref[0]
    for _, y_ref, c0, n, sl in blocks():
        y = z_scr[sl, 0:width] + _dot(h_scr[sl, :].astype(BF16), w2)
        y = jax.nn.gelu(y + dsk * u_scr[sl, :].astype(F32)).astype(BF16)
        yn = _dot(y, pt)
        for tt in range(t):
            slab = yn[:, tt * LANES:(tt + 1) * LANES].astype(BF16).reshape(n, bsz, LANES)
            y_ref[tt, c0:c0 + n] = y_ref[tt, c0:c0 + n] + slab


def _s5_lane_perm(t, hch):
    ng = LANES // hch
    pg = np.zeros((ng, t, LANES, t, hch), np.float32)
    for g in range(ng):
        for tt in range(t):
            pg[g, tt, g * hch + np.arange(hch), tt, np.arange(hch)] = 1.0
    return pg.reshape(ng, t * LANES, t * hch)


def _s5_core(hc, hl, tables, row_blk=512):
    w1, w2, dec, dsk = tables
    t, ncc, bsz, d = hc.shape
    ncl = hl.shape[1]
    width = w2.shape[2]
    hch = width // t
    ng = LANES // hch
    rows = (ncc + ncl) * bsz
    pg = _s5_lane_perm(t, hch)
    kern = functools.partial(_s5_kernel, bsz=bsz, ncc=ncc, ncl=ncl, width=width, row_blk=row_blk)
    gmap = lambda j, q: (j * ng + q, 0, 0)
    qmap = lambda j, q: (q, 0, 0)
    once = pl.Buffered(1)
    return pl.pallas_call(
        kern,
        grid=(d // LANES, ng),
        in_specs=[pl.BlockSpec((t, ncc, bsz, LANES), lambda j, q: (0, 0, 0, j), pipeline_mode=once),
                  pl.BlockSpec((t, ncl, bsz, LANES), lambda j, q: (0, 0, 0, j), pipeline_mode=once),
                  pl.BlockSpec((1,) + pg.shape[1:], qmap), pl.BlockSpec((1, t * hch, t * LANES), qmap),
                  pl.BlockSpec((1,) + w1.shape[1:], gmap), pl.BlockSpec((1,) + w2.shape[1:], gmap),
                  pl.BlockSpec((1,) + dec.shape[1:], gmap), pl.BlockSpec((1,) + dsk.shape[1:], gmap)],
        out_specs=[pl.BlockSpec((t, ncc, bsz, LANES), lambda j, q: (0, 0, 0, j), pipeline_mode=once),
                   pl.BlockSpec((t, ncl, bsz, LANES), lambda j, q: (0, 0, 0, j), pipeline_mode=once)],
        out_shape=[jax.ShapeDtypeStruct(hc.shape, BF16), jax.ShapeDtypeStruct(hl.shape, BF16)],
        scratch_shapes=[pltpu.VMEM((rows, w1.shape[2]), F32), pltpu.VMEM((rows, 4 * LANES), F32),
                        pltpu.VMEM((rows, width), BF16)],
        compiler_params=_params("parallel", "arbitrary"),
        name="s5_core",
    )(hc, hl, jnp.asarray(pg, dtype=BF16), jnp.asarray(pg.transpose(0, 2, 1), dtype=BF16), w1, w2, dec, dsk)


def _time_major(hm, bsz, t):
    n, d = hm.shape
    return jnp.transpose(hm.reshape(bsz, n // bsz // t, t, d), (2, 1, 0, 3))


def _token_major(y):
    t, nc, bsz, d = y.shape
    return jnp.transpose(y, (2, 1, 0, 3)).reshape(bsz * nc * t, d)


def _mixout_kernel(y_ref, w_ref, x_ref, mod_ref, o_ref, *, glu):
    r = _dot(y_ref[...], w_ref[...])
    if glu:
        half = r.shape[-1] // 2
        r = r[:, :half] * jax.nn.sigmoid(r[:, half:])
    o_ref[...] = x_ref[...] + mod_ref[0, 2:3, :] * r


def _mixout(y, w, x, mod, tile, tiles_per_mod, glu):
    n, d = x.shape
    k, nn = w.shape
    return pl.pallas_call(
        functools.partial(_mixout_kernel, glu=glu),
        grid=(n // tile,),
        in_specs=[pl.BlockSpec((tile, k), lambda i: (i, 0)),
                  pl.BlockSpec((k, nn), lambda i: (0, 0)),
                  pl.BlockSpec((tile, d), lambda i: (i, 0)),
                  pl.BlockSpec((1, 6, d), lambda i: (i // tiles_per_mod, 0, 0))],
        out_specs=pl.BlockSpec((tile, d), lambda i: (i, 0)),
        out_shape=jax.ShapeDtypeStruct((n, d), F32),
        compiler_params=_params("parallel"),
        name="mixer_out",
    )(y, w, x, mod)


def _mlp_kernel(x_ref, nw_ref, mod_ref, w1_ref, w2_ref, fw_ref, o_ref, *, ff_blk, final):
    x = x_ref[...]
    h = _norm_mod(x, nw_ref[...], mod_ref[0, 3:4, :], mod_ref[0, 4:5, :]).astype(BF16)
    acc = jnp.zeros(x.shape, F32)
    for f0 in range(0, w1_ref.shape[1], ff_blk):
        a = jnp.maximum(_dot(h, w1_ref[:, f0:f0 + ff_blk]), 0.0)
        acc = acc + _dot((a * a).astype(BF16), w2_ref[f0:f0 + ff_blk, :])
    y = x + mod_ref[0, 5:6, :] * acc
    if final:
        y = y * lax.rsqrt(jnp.mean(y * y, axis=-1, keepdims=True) + NORM_EPS) * fw_ref[...]
    o_ref[...] = y


def _mlp(x, nw, mod, w1, w2, fw, tile, tiles_per_mod, final):
    n, d = x.shape
    ff = w1.shape[1]
    return pl.pallas_call(
        functools.partial(_mlp_kernel, ff_blk=min(1024, ff), final=final),
        grid=(n // tile,),
        in_specs=[pl.BlockSpec((tile, d), lambda i: (i, 0)),
                  pl.BlockSpec((1, d), lambda i: (0, 0)),
                  pl.BlockSpec((1, 6, d), lambda i: (i // tiles_per_mod, 0, 0)),
                  pl.BlockSpec((d, ff), lambda i: (0, 0)),
                  pl.BlockSpec((ff, d), lambda i: (0, 0)),
                  pl.BlockSpec((1, d), lambda i: (0, 0))],
        out_specs=pl.BlockSpec((tile, d), lambda i: (i, 0)),
        out_shape=jax.ShapeDtypeStruct((n, d), F32),
        compiler_params=_params("parallel"),
        name="mlp",
    )(x, nw.reshape(1, d), mod, w1, w2, fw.reshape(1, d))


def _gdn_in_kernel(x_ref, nw_ref, mod_ref, w_ref, wg_ref, cw_ref, gp_ref, p_ref, g_ref, gt_ref, acc_scr,
                   *, period, col0, qk, vd, tn, hv, hdim):
    tb = x_ref.shape[0]
    h = _norm_mod(x_ref[...], nw_ref[...], mod_ref[0, 0:1, :], mod_ref[0, 1:2, :])
    hb = h.astype(BF16)
    gl = _dot(h, wg_ref[...], HIGHEST)
    lane = lax.broadcasted_iota(jnp.int32, (1, LANES), 1)
    beta = jax.nn.sigmoid(gl)
    t = gl + gp_ref[1:2, :]
    g = gp_ref[0:1, :] * (jnp.maximum(t, 0.0) + jnp.log1p(jnp.exp(-jnp.abs(t))))
    r = lax.broadcasted_iota(jnp.int32, (tb, tb), 0)
    c = lax.broadcasted_iota(jnp.int32, (tb, tb), 1)
    same = _floor_pow2(r, GDN_CHUNK) == _floor_pow2(c, GDN_CHUNK)
    gcf = _dot(jnp.where(same & (c <= r), 1.0, 0.0), g, HIGHEST)
    gcb = _dot(jnp.where(same & (c >= r), 1.0, 0.0), g, HIGHEST)
    out = jnp.where(lane < 2 * hv, beta, jnp.where(lane < 3 * hv, gcf, jnp.where(lane < 4 * hv, gcb, 0.0)))
    g_ref[...] = out
    for cc in range(tb // GDN_CHUNK):
        gt_ref[cc] = out[cc * GDN_CHUNK:(cc + 1) * GDN_CHUNK, :].T[2 * hv:4 * hv, :]

    assert tb % period == 0
    row = lax.broadcasted_iota(jnp.int32, (period, 1), 0)
    edge = [(tap, ((row + tap - 2 >= 0) & (row + tap - 2 < period)).astype(F32)) for tap in (0, 1, 3, 4)]

    for bi, o0 in enumerate(range(0, p_ref.shape[1], tn)):
        c0 = col0 + o0
        acc = _dot(hb, w_ref[:, c0:c0 + tn])
        if qk <= c0 < qk + vd:
            p_ref[:, o0:o0 + tn] = acc.astype(BF16)
            continue
        acc_ref = acc_scr.at[bi % 2]
        acc_ref[...] = acc
        scale = hdim ** -0.5 if c0 < qk else (1.0 if c0 < 2 * qk + vd else None)
        for h0 in range(0, tn, hdim):
            cw = cw_ref[:, c0 + h0:c0 + h0 + hdim]
            taps = [(tap, ok * cw[tap:tap + 1, :]) for tap, ok in edge]
            for r0 in range(0, tb, period):
                a = acc_ref[r0:r0 + period, h0:h0 + hdim]
                y = a * cw[2:3, :]
                for tap, wt in taps:
                    y = y + pltpu.roll(a, (period - (tap - 2)) % period, 0) * wt
                y = _silu(y)
                if scale is not None:
                    y = y * (lax.rsqrt(jnp.sum(y * y, axis=-1, keepdims=True) + NORM_EPS) * scale)
                p_ref[r0:r0 + period, o0 + h0:o0 + h0 + hdim] = y.astype(BF16)


def _gdn_in(x, nw, mod, w_main, w_gate, cw, gp, tile, tiles_per_mod, period, col0, tn, qk, vd, hv, hdim):
    n, d = x.shape
    n_main = w_main.shape[1]
    kern = functools.partial(_gdn_in_kernel, period=period, col0=col0, qk=qk, vd=vd, tn=tn, hv=hv, hdim=hdim)
    return pl.pallas_call(
        kern,
        grid=(n // tile,),
        in_specs=[pl.BlockSpec((tile, d), lambda i: (i, 0)),
                  pl.BlockSpec((1, d), lambda i: (0, 0)),
                  pl.BlockSpec((1, 6, d), lambda i: (i // tiles_per_mod, 0, 0)),
                  pl.BlockSpec((d, n_main), lambda i: (0, 0)),
                  pl.BlockSpec((d, LANES), lambda i: (0, 0)),
                  pl.BlockSpec((8, n_main), lambda i: (0, 0)),
                  pl.BlockSpec((2, LANES), lambda i: (0, 0))],
        out_specs=[pl.BlockSpec((tile, n_main - col0), lambda i: (i, 0)),
                   pl.BlockSpec((tile, LANES), lambda i: (i, 0)),
                   pl.BlockSpec((tile // GDN_CHUNK, 2 * hv, GDN_CHUNK), lambda i: (i, 0, 0))],
        out_shape=[jax.ShapeDtypeStruct((n, n_main - col0), BF16),
                   jax.ShapeDtypeStruct((n, LANES), F32),
                   jax.ShapeDtypeStruct((n // GDN_CHUNK, 2 * hv, GDN_CHUNK), F32)],
        scratch_shapes=[pltpu.VMEM((2, tile, tn), F32)],
        compiler_params=_params("parallel"),
        name="gdn_in",
    )(x, nw.reshape(1, d), mod, w_main, w_gate, cw, gp)


def _gdn_masks(ch, rep):
    nco = 2 * rep
    r, c = np.meshgrid(np.arange(ch), np.arange(ch), indexing="ij")
    per_dir = []
    for lower in (True, False):
        m = [(c <= r) if lower else (c >= r), (c < r) if lower else (c > r)]
        b = 1
        while b < ch:
            same = (r // (2 * b)) == (c // (2 * b))
            hi_r, hi_c = (r % (2 * b)) >= b, (c % (2 * b)) >= b
            m.append(same & ((hi_r & ~hi_c) if lower else (~hi_r & hi_c)))
            b *= 2
        per_dir.append(np.stack(m))
    cm = np.concatenate([per_dir[j // rep] for j in range(nco)], axis=2).astype(np.float32)
    blk = np.arange(nco * ch) // ch
    return cm, (blk[:, None] == blk[None, :]).astype(np.float32)


def _gdn_kernel(q_ref, k_ref, v_ref, z_ref, kc_ref, vc_ref, g_ref, gt_ref, gc_ref, gct_ref, ow_ref, cm_ref, lv_ref, bd_ref,
                o_ref, of_s, ob_s, st_s, uw_s, in_s, kt_s, *, hv, khb, rep, hdim, ncc, ncl, unroll):
    ch = GDN_CHUNK
    nco = 2 * rep
    nlvl = lv_ref.shape[0]
    kh0 = pl.program_id(1) * khb
    lane = lax.broadcasted_iota(jnp.int32, (1, LANES), 1)
    lcat = lax.broadcasted_iota(jnp.int32, (1, nco * ch), 1)

    def lanes_of(khl, j):
        head = (kh0 + khl) * rep + j % rep
        return (j // rep) * hv + head, (2 + j // rep) * hv + head

    def column(gt, idx):
        return jnp.sum(jnp.where(lane == idx, gt, 0.0), axis=1, keepdims=True)

    def spread(cols):
        out = jnp.broadcast_to(cols[nco - 1], (ch, nco * ch))
        for j in range(nco - 2, -1, -1):
            out = jnp.where(lcat < (j + 1) * ch, cols[j], out)
        return out

    def block_diag(xb):
        return jnp.concatenate([xb] * nco, axis=0) * bd_ref[...]

    def prepare(insts):
        st = []
        for khl, k_r, q_r, v_r, g_r, gt_r, c, slot in insts:
            rows = pl.ds(pl.multiple_of(c * ch, ch), ch)
            kb = k_r[rows, khl * hdim:(khl + 1) * hdim]
            kcat = jnp.concatenate([kb] * nco, axis=0)
            if q_r is None:
                kkc, qkc = _dot_nt(kb, kcat), None
            else:
                prod = _dot_nt(jnp.concatenate([kb, q_r[rows, khl * hdim:(khl + 1) * hdim]], axis=0), kcat)
                kkc, qkc = prod[:ch], prod[ch:]
            gtile = g_r[rows, :]
            betas, gcols, grows = [], [], []
            for j in range(nco):
                bl, gl = lanes_of(khl, j)
                betas.append(column(gtile, bl))
                gcols.append(column(gtile, gl))
                grows.append(gt_r[c, pl.ds(gl - 2 * hv, 1), :])
            dlog = spread(gcols) - jnp.concatenate(grows, axis=1)
            decay = jnp.exp(jnp.where(cm_ref[0] > 0.0, dlog, -jnp.inf))
            ac = spread(betas) * kkc * decay * cm_ref[1]
            xb = ((cm_ref[0] - cm_ref[1]) - ac * cm_ref[2]).astype(BF16)
            intra = None if qkc is None else (qkc * decay * cm_ref[0]).astype(BF16)
            st.append([rows, kb, intra, betas, gcols, None, ac.astype(BF16), xb])
        for lvl in range(1, nlvl):
            ps = [_dot(s[6] * lv_ref[lvl], block_diag(s[7])) for s in st]
            for s, p in zip(st, ps):
                s[7] = s[7] - _dot(s[7], block_diag(p.astype(BF16))).astype(BF16)
        outs = []
        for (khl, k_r, q_r, v_r, g_r, gt_r, c, slot), (rows, kb, intra, betas, gcols, _, ac, xc) in zip(insts, st):
            kf = kb.astype(F32)
            rst = []
            for j in range(nco):
                v0 = (khl * rep + j % rep) * hdim
                vf = v_r[rows, v0:v0 + hdim].astype(F32)
                rst.append(jnp.concatenate([vf * betas[j], kf * (betas[j] * jnp.exp(gcols[j]))], axis=1))
            uw = _dot(block_diag(xc), jnp.concatenate(rst, axis=0).astype(BF16))
            outs.append((uw, kf.T.astype(BF16), intra))
        for (khl, k_r, q_r, v_r, g_r, gt_r, c, slot), (uw, kt, intra) in zip(insts, outs):
            for j in range(nco):
                uw_s[slot, khl * nco + j] = uw[j * ch:(j + 1) * ch, :].astype(BF16)
                if intra is not None:
                    in_s[c, khl * nco + j] = intra[:, j * ch:(j + 1) * ch]
            kt_s[slot, khl] = kt

    def advance(items):
        mid = []
        for khl, j, q_r, g_r, gt_r, c, slot in items:
            rows = pl.ds(pl.multiple_of(c * ch, ch), ch)
            _, gl = lanes_of(khl, j)
            gcol = column(g_r[rows, :], gl)
            grow = gt_r[c, pl.ds(gl - 2 * hv, 1), :]
            glast = grow[:, ch - 1:ch] if j < rep else grow[:, 0:1]
            sidx = khl * nco + j
            s = st_s[sidx]
            lhs = uw_s[slot, sidx, :, hdim:]
            if q_r is not None:
                qg = q_r[rows, khl * hdim:(khl + 1) * hdim].astype(F32) * jnp.exp(gcol)
                lhs = jnp.concatenate([lhs, qg.astype(BF16)], axis=0)
            mid.append((rows, gcol, glast, sidx, s, _dot(lhs, s.astype(BF16))))
        res = []
        for (khl, j, q_r, g_r, gt_r, c, slot), (rows, gcol, glast, sidx, s, ws) in zip(items, mid):
            vnew = uw_s[slot, sidx, :, :hdim].astype(F32) - ws[:ch]
            snew = s * jnp.exp(glast) + _dot(kt_s[slot, khl], (vnew * jnp.exp(glast - gcol)).astype(BF16))
            o = None if q_r is None else ws[ch:] + _dot(in_s[c, sidx], vnew.astype(BF16))
            res.append((snew, o))
        for (khl, j, q_r, g_r, gt_r, c, slot), (rows, gcol, glast, sidx, s, ws), (snew, o) in zip(items, mid, res):
            st_s[sidx] = snew
            if o is not None:
                v0 = (khl * rep + j % rep) * hdim
                (of_s if j < rep else ob_s)[rows, v0:v0 + hdim] = o.astype(BF16)

    st_s[...] = jnp.zeros(st_s.shape, F32)
    uc = min(unroll, ncc)

    def prep_ctx(i, carry):
        prepare([(khl, kc_ref, None, vc_ref, gc_ref, gct_ref, i * uc + t, i * uc + t)
                 for t in range(uc) for khl in range(khb)])
        return carry

    def prep_lat(i, carry):
        prepare([(khl, k_ref, q_ref, v_ref, g_ref, gt_ref, i * unroll + t, ncc + i * unroll + t)
                 for t in range(unroll) for khl in range(khb)])
        return carry

    def adv_ctx(i, carry):
        advance([(khl, j, None, gc_ref, gct_ref, i if j < rep else ncc - 1 - i, i if j < rep else ncc - 1 - i)
                 for khl in range(khb) for j in range(nco)])
        return carry

    def adv_lat(i, carry):
        advance([(khl, j, q_ref, g_ref, gt_ref, i if j < rep else ncl - 1 - i,
                  ncc + (i if j < rep else ncl - 1 - i)) for khl in range(khb) for j in range(nco)])
        return carry

    lax.fori_loop(0, ncc // uc, prep_ctx, 0)
    lax.fori_loop(0, ncl // unroll, prep_lat, 0)
    lax.fori_loop(0, ncc, adv_ctx, 0)
    lax.fori_loop(0, ncl, adv_lat, 0)

    blk = min(256, ncl * ch)
    ow = ow_ref[...]
    for r0 in range(0, ncl * ch, blk):
        for v0 in range(0, khb * rep * hdim, hdim):
            o = of_s[r0:r0 + blk, v0:v0 + hdim].astype(F32) + ob_s[r0:r0 + blk, v0:v0 + hdim].astype(F32)
            o = o * lax.rsqrt(jnp.mean(o * o, axis=-1, keepdims=True) + NORM_EPS) * ow
            o_ref[r0:r0 + blk, v0:v0 + hdim] = (o * _silu(z_ref[r0:r0 + blk, v0:v0 + hdim].astype(F32))).astype(BF16)


def _gdn_core(p_lat, p_ctx, g_lat, gt_lat, g_ctx, gt_ctx, onorm_w, bsz, hk, hv, hdim, khb=GDN_KEY_HEADS_PER_STEP):
    l = p_lat.shape[0] // bsz
    lc = p_ctx.shape[0] // bsz
    ch = GDN_CHUNK
    ncc, ncl = lc // ch, l // ch
    rep = hv // hk
    nco = 2 * rep
    kw, vw = khb * hdim, khb * rep * hdim
    qk, vd = hk * hdim, hv * hdim
    unroll = min(GDN_PREP_UNROLL, ncl)
    assert hk % khb == 0 and ncl % unroll == 0 and ncc % min(unroll, ncc) == 0
    assert qk % vw == 0 and vd % vw == 0
    kern = functools.partial(_gdn_kernel, hv=hv, khb=khb, rep=rep, hdim=hdim, ncc=ncc, ncl=ncl, unroll=unroll)
    kcol, vcol, zcol = (qk + vd) // kw, (2 * qk + vd) // vw, qk // vw
    cm, bd = _gdn_masks(ch, rep)
    return pl.pallas_call(
        kern,
        grid=(bsz, hk // khb),
        in_specs=[pl.BlockSpec((l, kw), lambda b, h: (b, h)),
                  pl.BlockSpec((l, kw), lambda b, h: (b, kcol + h)),
                  pl.BlockSpec((l, vw), lambda b, h: (b, vcol + h)),
                  pl.BlockSpec((l, vw), lambda b, h: (b, zcol + h), pipeline_mode=pl.Buffered(1)),
                  pl.BlockSpec((lc, kw), lambda b, h: (b, h)),
                  pl.BlockSpec((lc, vw), lambda b, h: (b, qk // vw + h)),
                  pl.BlockSpec((l, LANES), lambda b, h: (b, 0)),
                  pl.BlockSpec((ncl, 2 * hv, ch), lambda b, h: (b, 0, 0)),
                  pl.BlockSpec((lc, LANES), lambda b, h: (b, 0)),
                  pl.BlockSpec((ncc, 2 * hv, ch), lambda b, h: (b, 0, 0)),
                  pl.BlockSpec((1, hdim), lambda b, h: (0, 0)),
                  pl.BlockSpec(cm[:3].shape, lambda b, h: (0, 0, 0)),
                  pl.BlockSpec(cm[2:].shape, lambda b, h: (0, 0, 0)),
                  pl.BlockSpec(bd.shape, lambda b, h: (0, 0))],
        out_specs=pl.BlockSpec((l, vw), lambda b, h: (b, h)),
        out_shape=jax.ShapeDtypeStruct((bsz * l, vd), BF16),
        scratch_shapes=[pltpu.VMEM((l, vw), BF16), pltpu.VMEM((l, vw), BF16),
                        pltpu.VMEM((khb * nco, hdim, hdim), F32),
                        pltpu.VMEM((ncc + ncl, khb * nco, ch, 2 * hdim), BF16),
                        pltpu.VMEM((ncl, khb * nco, ch, ch), BF16),
                        pltpu.VMEM((ncc + ncl, khb, hdim, ch), BF16)],
        compiler_params=_params("parallel", "arbitrary"),
        name="gdn_core",
    )(p_lat, p_lat, p_lat, p_lat, p_ctx, p_ctx, g_lat, gt_lat, g_ctx, gt_ctx, onorm_w.reshape(1, hdim),
      jnp.asarray(cm[:3]), jnp.asarray(cm[2:], dtype=BF16), jnp.asarray(bd, dtype=BF16))


def kernel(x, c, ctx, c_ctx, ada_w, ada_b, norm1_w, norm2_w, mlp_w1, mlp_w2, s5_lam_re, s5_lam_im, s5_log_dt, s5_b_re, s5_b_im, s5_c_re, s5_c_im, s5_d, s5_w_glu, gdn_w_in, gdn_conv_w, gdn_a_log, gdn_dt_bias, gdn_onorm_w, gdn_w_out, final_norm_w):
    bsz, l, d = x.shape
    lc = ctx.shape[1]
    depth = ada_w.shape[0]
    assert depth == 2 and s5_lam_re.shape[0] == 1 and gdn_w_in.shape[0] == 1
    groups = s5_lam_re.shape[2]
    hv = gdn_a_log.shape[2]
    hdim = gdn_onorm_w.shape[1]
    vd = gdn_w_out.shape[1]
    qk = (gdn_conv_w.shape[2] - vd) // 2
    hk = qk // hdim
    assert l % GRID_W == 0 and l % GDN_CHUNK == 0 and lc % GDN_CHUNK == 0
    assert l % S5_CHUNK == 0 and lc % S5_CHUNK == 0 and 4 * hv <= LANES

    tb_l, tb_c = min(256, l), lc
    tm_l, tm_c = min(512, l), min(512, bsz * lc)
    tx_l = min(1024, l)
    xl = x.reshape(bsz * l, d)
    xc = ctx.reshape(bsz * lc, d)

    rows = -(-(bsz + 1) // 8) * 8
    cvec = jnp.zeros((rows, d), F32).at[:bsz].set(c).at[bsz].set(c_ctx)
    mod = _ada_mod(cvec, ada_w, ada_b).reshape(depth, rows, 6, d)
    big = 1 << 30

    mod_l, mod_c = mod[0, :bsz], mod[0, bsz:bsz + 1]
    hl = _prenorm(xl, norm1_w[0], mod_l, tb_l, l // tb_l)
    hc = _prenorm(xc, norm1_w[0], mod_c, tb_c, big)
    tables = _s5_tables(s5_lam_re[0], s5_lam_im[0], s5_log_dt[0], s5_b_re[0], s5_b_im[0],
                        s5_c_re[0], s5_c_im[0], s5_d[0], S5_CHUNK)
    yc, yl = _s5_core(_time_major(hc, bsz, S5_CHUNK), _time_major(hl, bsz, S5_CHUNK), tables)
    w_glu = s5_w_glu[0].astype(BF16)
    w1, w2 = mlp_w1[0].astype(BF16), mlp_w2[0].astype(BF16)
    xl = _mixout(_token_major(yl), w_glu, xl, mod_l, tx_l, l // tx_l, True)
    xc = _mixout(_token_major(yc), w_glu, xc, mod_c, tm_c, big, True)
    xl = _mlp(xl, norm2_w[0], mod_l, w1, w2, final_norm_w, tm_l, l // tm_l, False)
    xc = _mlp(xc, norm2_w[0], mod_c, w1, w2, final_norm_w, tm_c, big, False)

    mod_l, mod_c = mod[1, :bsz], mod[1, bsz:bsz + 1]
    w_in = gdn_w_in[0]
    n_main = 2 * qk + 2 * vd
    w_main = w_in[:, :n_main].astype(BF16)
    w_gate = jnp.pad(w_in[:, n_main:], ((0, 0), (0, LANES - 4 * hv)))
    conv_w = gdn_conv_w[0].astype(F32)
    cw = jnp.concatenate([conv_w[:, :qk], jnp.zeros((conv_w.shape[0], vd), F32), conv_w[:, qk:]], axis=1)
    cw = jnp.pad(cw, ((0, 8 - cw.shape[0]), (0, 0)))
    neg_a = jnp.pad(-jnp.exp(gdn_a_log[0].astype(F32)).reshape(-1), (2 * hv, LANES - 4 * hv))
    dtb = jnp.pad(gdn_dt_bias[0].astype(F32).reshape(-1), (2 * hv, LANES - 4 * hv))
    gp = jnp.stack([neg_a, dtb])
    tn = min(512, qk)
    p_l, g_l, gt_l = _gdn_in(xl, norm1_w[1], mod_l, w_main, w_gate, cw, gp, tb_l, l // tb_l, GRID_W,
                             0, tn, qk, vd, hv, hdim)
    p_c, g_c, gt_c = _gdn_in(xc, norm1_w[1], mod_c, w_main, w_gate, cw, gp, tb_c, big, lc,
                             qk + vd, tn, qk, vd, hv, hdim)
    gated = _gdn_core(p_l, p_c, g_l, gt_l, g_c, gt_c, gdn_onorm_w[0].astype(F32), bsz, hk, hv, hdim)
    xl = _mixout(gated, gdn_w_out[0].astype(BF16), xl, mod_l, tx_l, l // tx_l, False)
    out = _mlp(xl, norm2_w[1], mod_l, mlp_w1[1].astype(BF16), mlp_w2[1].astype(BF16), final_norm_w,
               tm_l, l // tm_l, True)
    return out.reshape(bsz, l, d)
```

```python
import functools
import math

import jax
import jax.numpy as jnp
import numpy as np
from jax import lax
from jax.experimental import pallas as pl
from jax.experimental.pallas import tpu as pltpu

F32 = jnp.float32
BF16 = jnp.bfloat16
HIGHEST = lax.Precision.HIGHEST

NORM_EPS = 1e-6
GRID_W = 64
GDN_CHUNK = 64
GDN_KEY_HEADS_PER_STEP = 2
GDN_PREP_UNROLL = 8
S5_CHUNK = 16
LANES = 128
VMEM_LIMIT = 56 * 1024 * 1024


def _dot(a, b, precision=None):
    return jnp.dot(a, b, preferred_element_type=F32, precision=precision)


def _dot_nt(a, b):
    return lax.dot_general(a, b, (((1,), (1,)), ((), ())), preferred_element_type=F32)


def _dot_tn(a, b):
    return lax.dot_general(a, b, (((0,), (0,)), ((), ())), preferred_element_type=F32)


def _floor_pow2(v, n):
    assert n & (n - 1) == 0
    return jnp.bitwise_and(v, -n)


def _silu(t):
    return t * jax.nn.sigmoid(t)


def _norm_mod(x, nw, shift, scale):
    y = x * lax.rsqrt(jnp.mean(x * x, axis=-1, keepdims=True) + NORM_EPS) * nw
    return y * (1.0 + scale) + shift


def _params(*sem):
    return pltpu.CompilerParams(dimension_semantics=sem, vmem_limit_bytes=VMEM_LIMIT)


def _ada_kernel(c_ref, w_ref, b_ref, o_ref):
    o_ref[0] = _dot(_silu(c_ref[...]), w_ref[0], HIGHEST) + b_ref[0]


def _ada_mod(cvec, ada_w, ada_b):
    depth, d, n = ada_w.shape
    rows = cvec.shape[0]
    tn = n // 4
    return pl.pallas_call(
        _ada_kernel,
        grid=(depth, n // tn),
        in_specs=[pl.BlockSpec((rows, d), lambda i, j: (0, 0)),
                  pl.BlockSpec((1, d, tn), lambda i, j: (i, 0, j)),
                  pl.BlockSpec((1, 1, tn), lambda i, j: (i, 0, j))],
        out_specs=pl.BlockSpec((1, rows, tn), lambda i, j: (i, 0, j)),
        out_shape=jax.ShapeDtypeStruct((depth, rows, n), F32),
        compiler_params=_params("arbitrary", "arbitrary"),
        name="ada_mod",
    )(cvec, ada_w, ada_b.reshape(depth, 1, n))


def _prenorm_kernel(x_ref, nw_ref, mod_ref, o_ref):
    h = _norm_mod(x_ref[...], nw_ref[...], mod_ref[0, 0:1, :], mod_ref[0, 1:2, :])
    o_ref[...] = h.astype(o_ref.dtype)


def _prenorm(x, nw, mod, tile, tiles_per_mod):
    n, d = x.shape
    return pl.pallas_call(
        _prenorm_kernel,
        grid=(n // tile,),
        in_specs=[pl.BlockSpec((tile, d), lambda i: (i, 0)),
                  pl.BlockSpec((1, d), lambda i: (0, 0)),
                  pl.BlockSpec((1, 6, d), lambda i: (i // tiles_per_mod, 0, 0))],
        out_specs=pl.BlockSpec((tile, d), lambda i: (i, 0)),
        out_shape=jax.ShapeDtypeStruct((n, d), BF16),
        compiler_params=_params("parallel"),
        name="prenorm",
    )(x, nw.reshape(1, d), mod)


def _s5_tables(lam_re, lam_im, log_dt, b_re, b_im, c_re, c_im, d_skip, t):
    _, g, p = lam_re.shape
    h = b_re.shape[-1]
    f = lambda a: a.astype(F32)
    lam_re, lam_im, b_re, b_im, c_re, c_im = map(f, (lam_re, lam_im, b_re, b_im, c_re, c_im))
    dt = jnp.exp(f(log_dt))[..., None]
    zr, zi = lam_re * dt, lam_im * dt
    n = jnp.arange(t + 1, dtype=F32)[:, None, None, None]
    mag = jnp.exp(n * zr)
    pr, pi = mag * jnp.cos(n * zi), mag * jnp.sin(n * zi)
    ar, ai = pr[1], pi[1]
    den = lam_re * lam_re + lam_im * lam_im
    fr = ((ar - 1.0) * lam_re + ai * lam_im) / den
    fi = (ai * lam_re - (ar - 1.0) * lam_im) / den
    bbr = fr[..., None] * b_re - fi[..., None] * b_im
    bbi = fr[..., None] * b_im + fi[..., None] * b_re
    car = c_re[None] * pr[:, :, :, None, :] - c_im[None] * pi[:, :, :, None, :]
    cai = c_re[None] * pi[:, :, :, None, :] + c_im[None] * pr[:, :, :, None, :]
    kern = jnp.einsum('ndgoq,dgqi->ndgoi', jnp.concatenate([car, -cai], axis=-1),
                      jnp.concatenate([bbr, bbi], axis=2), precision=HIGHEST)
    ti = jnp.arange(t)
    lag = np.arange(t)[None, :] - np.arange(t)[:, None]
    place = np.stack([(lag == n) for n in range(t + 1)] + [(-lag == n) for n in range(t + 1)])
    place = place.reshape(2, t + 1, t, t).astype(np.float32)
    intra = jnp.einsum('dnst,ndgoi->gsito', place, kern, precision=HIGHEST).reshape(g, t * h, t * h)

    def inject(d, powers):
        er, ei = pr[powers, d], pi[powers, d]
        re = er[..., None] * bbr[d][None] - ei[..., None] * bbi[d][None]
        im = er[..., None] * bbi[d][None] + ei[..., None] * bbr[d][None]
        tr = lambda a: jnp.transpose(a, (1, 0, 3, 2)).reshape(g, t * h, p)
        return tr(re), tr(im)

    pad = lambda a: jnp.pad(a, ((0, 0), (0, 0), (0, LANES - p)))
    in_f = inject(0, t - 1 - ti)
    in_b = inject(1, ti)
    w1 = jnp.concatenate([intra] + [pad(a) for a in (*in_f, *in_b)], axis=-1)

    def readout(d, powers):
        tr = lambda a: jnp.transpose(a, (1, 3, 0, 2)).reshape(g, p, t * h)
        return tr(car[powers, d]), tr(-cai[powers, d])

    padr = lambda a: jnp.pad(a, ((0, 0), (0, LANES - p), (0, 0)))
    w2 = jnp.concatenate([padr(a) for a in (*readout(0, ti + 1), *readout(1, t - ti))], axis=1)
    dec = jnp.stack([pr[t, 0], pi[t, 0], pr[t, 1], pi[t, 1]], axis=1)
    dec = jnp.pad(dec, ((0, 0), (0, 4), (0, LANES - p)))
    dsk = jnp.tile(f(d_skip).reshape(g, 1, h), (1, t, 1)).reshape(g, 1, t * h)
    return w1.astype(BF16), w2.astype(BF16), dec, dsk


def _s5_kernel(hc_ref, hl_ref, pg_ref, pt_ref, w1_ref, w2_ref, dec_ref, dsk_ref, yc_ref, yl_ref,
               z_scr, h_scr, u_scr, *, bsz, ncc, ncl, width, row_blk):
    rc, rl = ncc * bsz, ncl * bsz
    t = hl_ref.shape[0]
    cb = max(1, row_blk // bsz)
    w1 = w1_ref[0]
    pg = pg_ref[0]

    @pl.when(pl.program_id(1) == 0)
    def _():
        yc_ref[...] = jnp.zeros(yc_ref.shape, BF16)
        yl_ref[...] = jnp.zeros(yl_ref.shape, BF16)

    def blocks():
        for h_ref, y_ref, base, nch in ((hc_ref, yc_ref, 0, ncc), (hl_ref, yl_ref, rc, ncl)):
            for c0 in range(0, nch, cb):
                n = min(cb, nch - c0)
                yield h_ref, y_ref, c0, n, slice(base + c0 * bsz, base + (c0 + n) * bsz)

    for h_ref, _, c0, n, sl in blocks():
        xcat = jnp.concatenate([h_ref[tt, c0:c0 + n].reshape(n * bsz, LANES) for tt in range(t)], axis=1)
        u = _dot(xcat, pg).astype(BF16)
        u_scr[sl, :] = u
        z_scr[sl, :] = _dot(u, w1)
    dec = dec_ref[0]
    zero = jnp.zeros((bsz, LANES), F32)
    cols = [width + k * LANES for k in range(5)]

    def make_step(are, aim, col, hcol):
        def step(row, carry):
            re, im = carry
            r = pl.multiple_of(row, bsz)
            h_scr[pl.ds(r, bsz), hcol:hcol + LANES] = re
            h_scr[pl.ds(r, bsz), hcol + LANES:hcol + 2 * LANES] = im
            s_re = z_scr[pl.ds(r, bsz), cols[col]:cols[col + 1]]
            s_im = z_scr[pl.ds(r, bsz), cols[col + 1]:cols[col + 2]]
            return are * re - aim * im + s_re, are * im + aim * re + s_im
        return step

    fstep = make_step(dec[0:1, :], dec[1:2, :], 0, 0)
    bstep = make_step(dec[2:3, :], dec[3:4, :], 2, 2 * LANES)
    lax.fori_loop(0, ncc + ncl, lambda c, s: fstep(c * bsz, s), (zero, zero))
    st = lax.fori_loop(0, ncc, lambda i, s: bstep((ncc - 1 - i) * bsz, s), (zero, zero))
    lax.fori_loop(0, ncl, lambda i, s: bstep((ncc + ncl - 1 - i) * bsz, s), st)

    w2 = w2_ref[0]
    dsk = dsk_ref[0]
    pt = pt_ref[0]
    for _, y_ref, c0, n, sl in blocks():
        y = z_scr[sl, 0:width] + _dot(h_scr[sl, :].astype(BF16), w2)
        y = jax.nn.gelu(y + dsk * u_scr[sl, :].astype(F32)).astype(BF16)
        yn = _dot(y, pt)
        for tt in range(t):
            slab = yn[:, tt * LANES:(tt + 1) * LANES].astype(BF16).reshape(n, bsz, LANES)
            y_ref[tt, c0:c0 + n] = y_ref[tt, c0:c0 + n] + slab


def _s5_lane_perm(t, hch):
    ng = LANES // hch
    pg = np.zeros((ng, t, LANES, t, hch), np.float32)
    for g in range(ng):
        for tt in range(t):
            pg[g, tt, g * hch + np.arange(hch), tt, np.arange(hch)] = 1.0
    return pg.reshape(ng, t * LANES, t * hch)


def _s5_core(hc, hl, tables, row_blk=512):
    w1, w2, dec, dsk = tables
    t, ncc, bsz, d = hc.shape
    ncl = hl.shape[1]
    width = w2.shape[2]
    hch = width // t
    ng = LANES // hch
    rows = (ncc + ncl) * bsz
    pg = _s5_lane_perm(t, hch)
    kern = functools.partial(_s5_kernel, bsz=bsz, ncc=ncc, ncl=ncl, width=width, row_blk=row_blk)
    gmap = lambda j, q: (j * ng + q, 0, 0)
    qmap = lambda j, q: (q, 0, 0)
    once = pl.Buffered(1)
    return pl.pallas_call(
        kern,
        grid=(d // LANES, ng),
        in_specs=[pl.BlockSpec((t, ncc, bsz, LANES), lambda j, q: (0, 0, 0, j), pipeline_mode=once),
                  pl.BlockSpec((t, ncl, bsz, LANES), lambda j, q: (0, 0, 0, j), pipeline_mode=once),
                  pl.BlockSpec((1,) + pg.shape[1:], qmap), pl.BlockSpec((1, t * hch, t * LANES), qmap),
                  pl.BlockSpec((1,) + w1.shape[1:], gmap), pl.BlockSpec((1,) + w2.shape[1:], gmap),
                  pl.BlockSpec((1,) + dec.shape[1:], gmap), pl.BlockSpec((1,) + dsk.shape[1:], gmap)],
        out_specs=[pl.BlockSpec((t, ncc, bsz, LANES), lambda j, q: (0, 0, 0, j), pipeline_mode=once),
                   pl.BlockSpec((t, ncl, bsz, LANES), lambda j, q: (0, 0, 0, j), pipeline_mode=once)],
        out_shape=[jax.ShapeDtypeStruct(hc.shape, BF16), jax.ShapeDtypeStruct(hl.shape, BF16)],
        scratch_shapes=[pltpu.VMEM((rows, w1.shape[2]), F32), pltpu.VMEM((rows, 4 * LANES), F32),
                        pltpu.VMEM((rows, width), BF16)],
        compiler_params=_params("parallel", "arbitrary"),
        name="s5_core",
    )(hc, hl, jnp.asarray(pg, dtype=BF16), jnp.asarray(pg.transpose(0, 2, 1), dtype=BF16), w1, w2, dec, dsk)


def _time_major(hm, bsz, t):
    n, d = hm.shape
    return jnp.transpose(hm.reshape(bsz, n // bsz // t, t, d), (2, 1, 0, 3))


def _token_major(y):
    t, nc, bsz, d = y.shape
    return jnp.transpose(y, (2, 1, 0, 3)).reshape(bsz * nc * t, d)


def _mixout_kernel(y_ref, w_ref, x_ref, mod_ref, o_ref, *, glu):
    r = _dot(y_ref[...], w_ref[...])
    if glu:
        half = r.shape[-1] // 2
        r = r[:, :half] * jax.nn.sigmoid(r[:, half:])
    o_ref[...] = x_ref[...] + mod_ref[0, 2:3, :] * r


def _mixout(y, w, x, mod, tile, tiles_per_mod, glu):
    n, d = x.shape
    k, nn = w.shape
    return pl.pallas_call(
        functools.partial(_mixout_kernel, glu=glu),
        grid=(n // tile,),
        in_specs=[pl.BlockSpec((tile, k), lambda i: (i, 0)),
                  pl.BlockSpec((k, nn), lambda i: (0, 0)),
                  pl.BlockSpec((tile, d), lambda i: (i, 0)),
                  pl.BlockSpec((1, 6, d), lambda i: (i // tiles_per_mod, 0, 0))],
        out_specs=pl.BlockSpec((tile, d), lambda i: (i, 0)),
        out_shape=jax.ShapeDtypeStruct((n, d), F32),
        compiler_params=_params("parallel"),
        name="mixer_out",
    )(y, w, x, mod)


def _mlp_kernel(x_ref, nw_ref, mod_ref, w1_ref, w2_ref, fw_ref, o_ref, *, ff_blk, final):
    x = x_ref[...]
    h = _norm_mod(x, nw_ref[...], mod_ref[0, 3:4, :], mod_ref[0, 4:5, :]).astype(BF16)
    acc = jnp.zeros(x.shape, F32)
    for f0 in range(0, w1_ref.shape[1], ff_blk):
        a = jnp.maximum(_dot(h, w1_ref[:, f0:f0 + ff_blk]), 0.0)
        acc = acc + _dot((a * a).astype(BF16), w2_ref[f0:f0 + ff_blk, :])
    y = x + mod_ref[0, 5:6, :] * acc
    if final:
        y = y * lax.rsqrt(jnp.mean(y * y, axis=-1, keepdims=True) + NORM_EPS) * fw_ref[...]
    o_ref[...] = y


def _mlp(x, nw, mod, w1, w2, fw, tile, tiles_per_mod, final):
    n, d = x.shape
    ff = w1.shape[1]
    return pl.pallas_call(
        functools.partial(_mlp_kernel, ff_blk=min(1024, ff), final=final),
        grid=(n // tile,),
        in_specs=[pl.BlockSpec((tile, d), lambda i: (i, 0)),
                  pl.BlockSpec((1, d), lambda i: (0, 0)),
                  pl.BlockSpec((1, 6, d), lambda i: (i // tiles_per_mod, 0, 0)),
                  pl.BlockSpec((d, ff), lambda i: (0, 0)),
                  pl.BlockSpec((ff, d), lambda i: (0, 0)),
                  pl.BlockSpec((1, d), lambda i: (0, 0))],
        out_specs=pl.BlockSpec((tile, d), lambda i: (i, 0)),
        out_shape=jax.ShapeDtypeStruct((n, d), F32),
        compiler_params=_params("parallel"),
        name="mlp",
    )(x, nw.reshape(1, d), mod, w1, w2, fw.reshape(1, d))


def _gdn_in_kernel(x_ref, nw_ref, mod_ref, w_ref, wg_ref, cw_ref, gp_ref, p_ref, g_ref, gt_ref, acc_scr,
                   *, period, col0, qk, vd, tn, hv, hdim):
    tb = x_ref.shape[0]
    h = _norm_mod(x_ref[...], nw_ref[...], mod_ref[0, 0:1, :], mod_ref[0, 1:2, :])
    hb = h.astype(BF16)
    gl = _dot(h, wg_ref[...], HIGHEST)
    lane = lax.broadcasted_iota(jnp.int32, (1, LANES), 1)
    beta = jax.nn.sigmoid(gl)
    t = gl + gp_ref[1:2, :]
    g = gp_ref[0:1, :] * (jnp.maximum(t, 0.0) + jnp.log1p(jnp.exp(-jnp.abs(t))))
    r = lax.broadcasted_iota(jnp.int32, (tb, tb), 0)
    c = lax.broadcasted_iota(jnp.int32, (tb, tb), 1)
    same = _floor_pow2(r, GDN_CHUNK) == _floor_pow2(c, GDN_CHUNK)
    gcf = _dot(jnp.where(same & (c <= r), 1.0, 0.0), g, HIGHEST)
    gcb = _dot(jnp.where(same & (c >= r), 1.0, 0.0), g, HIGHEST)
    out = jnp.where(lane < 2 * hv, beta, jnp.where(lane < 3 * hv, gcf, jnp.where(lane < 4 * hv, gcb, 0.0)))
    g_ref[...] = out
    for cc in range(tb // GDN_CHUNK):
        gt_ref[cc] = out[cc * GDN_CHUNK:(cc + 1) * GDN_CHUNK, :].T[2 * hv:4 * hv, :]

    assert tb % period == 0
    row = lax.broadcasted_iota(jnp.int32, (period, 1), 0)
    edge = [(tap, ((row + tap - 2 >= 0) & (row + tap - 2 < period)).astype(F32)) for tap in (0, 1, 3, 4)]

    for bi, o0 in enumerate(range(0, p_ref.shape[1], tn)):
        c0 = col0 + o0
        acc = _dot(hb, w_ref[:, c0:c0 + tn])
        if qk <= c0 < qk + vd:
            p_ref[:, o0:o0 + tn] = acc.astype(BF16)
            continue
        acc_ref = acc_scr.at[bi % 2]
        acc_ref[...] = acc
        scale = hdim ** -0.5 if c0 < qk else (1.0 if c0 < 2 * qk + vd else None)
        for h0 in range(0, tn, hdim):
            cw = cw_ref[:, c0 + h0:c0 + h0 + hdim]
            taps = [(tap, ok * cw[tap:tap + 1, :]) for tap, ok in edge]
            for r0 in range(0, tb, period):
                a = acc_ref[r0:r0 + period, h0:h0 + hdim]
                y = a * cw[2:3, :]
                for tap, wt in taps:
                    y = y + pltpu.roll(a, (period - (tap - 2)) % period, 0) * wt
                y = _silu(y)
                if scale is not None:
                    y = y * (lax.rsqrt(jnp.sum(y * y, axis=-1, keepdims=True) + NORM_EPS) * scale)
                p_ref[r0:r0 + period, o0 + h0:o0 + h0 + hdim] = y.astype(BF16)


def _gdn_in(x, nw, mod, w_main, w_gate, cw, gp, tile, tiles_per_mod, period, col0, tn, qk, vd, hv, hdim):
    n, d = x.shape
    n_main = w_main.shape[1]
    kern = functools.partial(_gdn_in_kernel, period=period, col0=col0, qk=qk, vd=vd, tn=tn, hv=hv, hdim=hdim)
    return pl.pallas_call(
        kern,
        grid=(n // tile,),
        in_specs=[pl.BlockSpec((tile, d), lambda i: (i, 0)),
                  pl.BlockSpec((1, d), lambda i: (0, 0)),
                  pl.BlockSpec((1, 6, d), lambda i: (i // tiles_per_mod, 0, 0)),
                  pl.BlockSpec((d, n_main), lambda i: (0, 0)),
                  pl.BlockSpec((d, LANES), lambda i: (0, 0)),
                  pl.BlockSpec((8, n_main), lambda i: (0, 0)),
                  pl.BlockSpec((2, LANES), lambda i: (0, 0))],
        out_specs=[pl.BlockSpec((tile, n_main - col0), lambda i: (i, 0)),
                   pl.BlockSpec((tile, LANES), lambda i: (i, 0)),
                   pl.BlockSpec((tile // GDN_CHUNK, 2 * hv, GDN_CHUNK), lambda i: (i, 0, 0))],
        out_shape=[jax.ShapeDtypeStruct((n, n_main - col0), BF16),
                   jax.ShapeDtypeStruct((n, LANES), F32),
                   jax.ShapeDtypeStruct((n // GDN_CHUNK, 2 * hv, GDN_CHUNK), F32)],
        scratch_shapes=[pltpu.VMEM((2, tile, tn), F32)],
        compiler_params=_params("parallel"),
        name="gdn_in",
    )(x, nw.reshape(1, d), mod, w_main, w_gate, cw, gp)


def _gdn_masks(ch, rep):
    nco = 2 * rep
    r, c = np.meshgrid(np.arange(ch), np.arange(ch), indexing="ij")
    per_dir = []
    for lower in (True, False):
        m = [(c <= r) if lower else (c >= r), (c < r) if lower else (c > r)]
        b = 1
        while b < ch:
            same = (r // (2 * b)) == (c // (2 * b))
            hi_r, hi_c = (r % (2 * b)) >= b, (c % (2 * b)) >= b
            m.append(same & ((hi_r & ~hi_c) if lower else (~hi_r & hi_c)))
            b *= 2
        per_dir.append(np.stack(m))
    cm = np.concatenate([per_dir[j // rep] for j in range(nco)], axis=2).astype(np.float32)
    blk = np.arange(nco * ch) // ch
    return cm, (blk[:, None] == blk[None, :]).astype(np.float32)


def _gdn_kernel(q_ref, k_ref, v_ref, z_ref, kc_ref, vc_ref, g_ref, gt_ref, gc_ref, gct_ref, ow_ref, cm_ref, lv_ref, bd_ref,
                o_ref, of_s, ob_s, st_s, uw_s, in_s, kt_s, *, hv, khb, rep, hdim, ncc, ncl, unroll):
    ch = GDN_CHUNK
    nco = 2 * rep
    nlvl = lv_ref.shape[0]
    kh0 = pl.program_id(1) * khb
    lane = lax.broadcasted_iota(jnp.int32, (1, LANES), 1)
    lcat = lax.broadcasted_iota(jnp.int32, (1, nco * ch), 1)

    def lanes_of(khl, j):
        head = (kh0 + khl) * rep + j % rep
        return (j // rep) * hv + head, (2 + j // rep) * hv + head

    def column(gt, idx):
        return jnp.sum(jnp.where(lane == idx, gt, 0.0), axis=1, keepdims=True)

    def spread(cols):
        out = jnp.broadcast_to(cols[nco - 1], (ch, nco * ch))
        for j in range(nco - 2, -1, -1):
            out = jnp.where(lcat < (j + 1) * ch, cols[j], out)
        return out

    def block_diag(xb):
        return jnp.concatenate([xb] * nco, axis=0) * bd_ref[...]

    def prepare(insts):
        st = []
        for khl, k_r, q_r, v_r, g_r, gt_r, c, slot in insts:
            rows = pl.ds(pl.multiple_of(c * ch, ch), ch)
            kb = k_r[rows, khl * hdim:(khl + 1) * hdim]
            kcat = jnp.concatenate([kb] * nco, axis=0)
            if q_r is None:
                kkc, qkc = _dot_nt(kb, kcat), None
            else:
                prod = _dot_nt(jnp.concatenate([kb, q_r[rows, khl * hdim:(khl + 1) * hdim]], axis=0), kcat)
                kkc, qkc = prod[:ch], prod[ch:]
            gtile = g_r[rows, :]
            betas, gcols, grows = [], [], []
            for j in range(nco):
                bl, gl = lanes_of(khl, j)
                betas.append(column(gtile, bl))
                gcols.append(column(gtile, gl))
                grows.append(gt_r[c, pl.ds(gl - 2 * hv, 1), :])
            dlog = spread(gcols) - jnp.concatenate(grows, axis=1)
            decay = jnp.exp(jnp.where(cm_ref[0] > 0.0, dlog, -jnp.inf))
            ac = spread(betas) * kkc * decay * cm_ref[1]
            xb = ((cm_ref[0] - cm_ref[1]) - ac * cm_ref[2]).astype(BF16)
            st.append([rows, kb, qkc, betas, gcols, decay, ac.astype(BF16), xb])
        for lvl in range(1, nlvl):
            ps = [_dot(s[6] * lv_ref[lvl], block_diag(s[7])) for s in st]
            for s, p in zip(st, ps):
                s[7] = s[7] - _dot(s[7], block_diag(p.astype(BF16))).astype(BF16)
        outs = []
        for (khl, k_r, q_r, v_r, g_r, gt_r, c, slot), (rows, kb, qkc, betas, gcols, decay, ac, xc) in zip(insts, st):
            kf = kb.astype(F32)
            rst = []
            for j in range(nco):
                v0 = (khl * rep + j % rep) * hdim
                vf = v_r[rows, v0:v0 + hdim].astype(F32)
                rst.append(jnp.concatenate([vf * betas[j], kf * (betas[j] * jnp.exp(gcols[j]))], axis=1))
            uw = _dot(block_diag(xc), jnp.concatenate(rst, axis=0).astype(BF16))
            intra = None if qkc is None else (qkc * decay * cm_ref[0]).astype(BF16)
            outs.append((uw, kf.T.astype(BF16), intra))
        for (khl, k_r, q_r, v_r, g_r, gt_r, c, slot), (uw, kt, intra) in zip(insts, outs):
            for j in range(nco):
                uw_s[slot, khl * nco + j] = uw[j * ch:(j + 1) * ch, :].astype(BF16)
                if intra is not None:
                    in_s[c, khl * nco + j] = intra[:, j * ch:(j + 1) * ch]
            kt_s[slot, khl] = kt

    def advance(items):
        mid = []
        for khl, j, q_r, g_r, gt_r, c, slot in items:
            rows = pl.ds(pl.multiple_of(c * ch, ch), ch)
            _, gl = lanes_of(khl, j)
            gcol = column(g_r[rows, :], gl)
            grow = gt_r[c, pl.ds(gl - 2 * hv, 1), :]
            glast = grow[:, ch - 1:ch] if j < rep else grow[:, 0:1]
            sidx = khl * nco + j
            s = st_s[sidx]
            lhs = uw_s[slot, sidx, :, hdim:]
            if q_r is not None:
                qg = q_r[rows, khl * hdim:(khl + 1) * hdim].astype(F32) * jnp.exp(gcol)
                lhs = jnp.concatenate([lhs, qg.astype(BF16)], axis=0)
            mid.append((rows, gcol, glast, sidx, s, _dot(lhs, s.astype(BF16))))
        res = []
        for (khl, j, q_r, g_r, gt_r, c, slot), (rows, gcol, glast, sidx, s, ws) in zip(items, mid):
            vnew = uw_s[slot, sidx, :, :hdim].astype(F32) - ws[:ch]
            snew = s * jnp.exp(glast) + _dot(kt_s[slot, khl], (vnew * jnp.exp(glast - gcol)).astype(BF16))
            o = None if q_r is None else ws[ch:] + _dot(in_s[c, sidx], vnew.astype(BF16))
            res.append((snew, o))
        for (khl, j, q_r, g_r, gt_r, c, slot), (rows, gcol, glast, sidx, s, ws), (snew, o) in zip(items, mid, res):
            st_s[sidx] = snew
            if o is not None:
                v0 = (khl * rep + j % rep) * hdim
                (of_s if j < rep else ob_s)[rows, v0:v0 + hdim] = o.astype(BF16)

    st_s[...] = jnp.zeros(st_s.shape, F32)
    uc = min(unroll, ncc)

    def prep_ctx(i, carry):
        prepare([(khl, kc_ref, None, vc_ref, gc_ref, gct_ref, i * uc + t, i * uc + t)
                 for t in range(uc) for khl in range(khb)])
        return carry

    def prep_lat(i, carry):
        prepare([(khl, k_ref, q_ref, v_ref, g_ref, gt_ref, i * unroll + t, ncc + i * unroll + t)
                 for t in range(unroll) for khl in range(khb)])
        return carry

    def adv_ctx(i, carry):
        advance([(khl, j, None, gc_ref, gct_ref, i if j < rep else ncc - 1 - i, i if j < rep else ncc - 1 - i)
                 for khl in range(khb) for j in range(nco)])
        return carry

    def adv_lat(i, carry):
        advance([(khl, j, q_ref, g_ref, gt_ref, i if j < rep else ncl - 1 - i,
                  ncc + (i if j < rep else ncl - 1 - i)) for khl in range(khb) for j in range(nco)])
        return carry

    lax.fori_loop(0, ncc // uc, prep_ctx, 0)
    lax.fori_loop(0, ncl // unroll, prep_lat, 0)
    lax.fori_loop(0, ncc, adv_ctx, 0)
    lax.fori_loop(0, ncl, adv_lat, 0)

    blk = min(256, ncl * ch)
    ow = ow_ref[...]
    for r0 in range(0, ncl * ch, blk):
        for v0 in range(0, khb * rep * hdim, hdim):
            o = of_s[r0:r0 + blk, v0:v0 + hdim].astype(F32) + ob_s[r0:r0 + blk, v0:v0 + hdim].astype(F32)
            o = o * lax.rsqrt(jnp.mean(o * o, axis=-1, keepdims=True) + NORM_EPS) * ow
            o_ref[r0:r0 + blk, v0:v0 + hdim] = (o * _silu(z_ref[r0:r0 + blk, v0:v0 + hdim].astype(F32))).astype(BF16)


def _gdn_core(p_lat, p_ctx, g_lat, gt_lat, g_ctx, gt_ctx, onorm_w, bsz, hk, hv, hdim, khb=GDN_KEY_HEADS_PER_STEP):
    l = p_lat.shape[0] // bsz
    lc = p_ctx.shape[0] // bsz
    ch = GDN_CHUNK
    ncc, ncl = lc // ch, l // ch
    rep = hv // hk
    nco = 2 * rep
    kw, vw = khb * hdim, khb * rep * hdim
    qk, vd = hk * hdim, hv * hdim
    unroll = min(GDN_PREP_UNROLL, ncl)
    assert hk % khb == 0 and ncl % unroll == 0 and ncc % min(unroll, ncc) == 0
    assert qk % vw == 0 and vd % vw == 0
    kern = functools.partial(_gdn_kernel, hv=hv, khb=khb, rep=rep, hdim=hdim, ncc=ncc, ncl=ncl, unroll=unroll)
    kcol, vcol, zcol = (qk + vd) // kw, (2 * qk + vd) // vw, qk // vw
    cm, bd = _gdn_masks(ch, rep)
    return pl.pallas_call(
        kern,
        grid=(bsz, hk // khb),
        in_specs=[pl.BlockSpec((l, kw), lambda b, h: (b, h)),
                  pl.BlockSpec((l, kw), lambda b, h: (b, kcol + h)),
                  pl.BlockSpec((l, vw), lambda b, h: (b, vcol + h)),
                  pl.BlockSpec((l, vw), lambda b, h: (b, zcol + h)),
                  pl.BlockSpec((lc, kw), lambda b, h: (b, h)),
                  pl.BlockSpec((lc, vw), lambda b, h: (b, qk // vw + h)),
                  pl.BlockSpec((l, LANES), lambda b, h: (b, 0)),
                  pl.BlockSpec((ncl, 2 * hv, ch), lambda b, h: (b, 0, 0)),
                  pl.BlockSpec((lc, LANES), lambda b, h: (b, 0)),
                  pl.BlockSpec((ncc, 2 * hv, ch), lambda b, h: (b, 0, 0)),
                  pl.BlockSpec((1, hdim), lambda b, h: (0, 0)),
                  pl.BlockSpec(cm[:3].shape, lambda b, h: (0, 0, 0)),
                  pl.BlockSpec(cm[2:].shape, lambda b, h: (0, 0, 0)),
                  pl.BlockSpec(bd.shape, lambda b, h: (0, 0))],
        out_specs=pl.BlockSpec((l, vw), lambda b, h: (b, h)),
        out_shape=jax.ShapeDtypeStruct((bsz * l, vd), BF16),
        scratch_shapes=[pltpu.VMEM((l, vw), BF16), pltpu.VMEM((l, vw), BF16),
                        pltpu.VMEM((khb * nco, hdim, hdim), F32),
                        pltpu.VMEM((ncc + ncl, khb * nco, ch, 2 * hdim), BF16),
                        pltpu.VMEM((ncl, khb * nco, ch, ch), BF16),
                        pltpu.VMEM((ncc + ncl, khb, hdim, ch), BF16)],
        compiler_params=_params("parallel", "arbitrary"),
        name="gdn_core",
    )(p_lat, p_lat, p_lat, p_lat, p_ctx, p_ctx, g_lat, gt_lat, g_ctx, gt_ctx, onorm_w.reshape(1, hdim),
      jnp.asarray(cm[:3]), jnp.asarray(cm[2:], dtype=BF16), jnp.asarray(bd, dtype=BF16))


def kernel(x, c, ctx, c_ctx, ada_w, ada_b, norm1_w, norm2_w, mlp_w1, mlp_w2, s5_lam_re, s5_lam_im, s5_log_dt, s5_b_re, s5_b_im, s5_c_re, s5_c_im, s5_d, s5_w_glu, gdn_w_in, gdn_conv_w, gdn_a_log, gdn_dt_bias, gdn_onorm_w, gdn_w_out, final_norm_w):
    bsz, l, d = x.shape
    lc = ctx.shape[1]
    depth = ada_w.shape[0]
    assert depth == 2 and s5_lam_re.shape[0] == 1 and gdn_w_in.shape[0] == 1
    groups = s5_lam_re.shape[2]
    hv = gdn_a_log.shape[2]
    hdim = gdn_onorm_w.shape[1]
    vd = gdn_w_out.shape[1]
    qk = (gdn_conv_w.shape[2] - vd) // 2
    hk = qk // hdim
    assert l % GRID_W == 0 and l % GDN_CHUNK == 0 and lc % GDN_CHUNK == 0
    assert l % S5_CHUNK == 0 and lc % S5_CHUNK == 0 and 4 * hv <= LANES

    tb_l, tb_c = min(256, l), lc
    tm_l, tm_c = min(512, l), min(512, bsz * lc)
    tx_l = min(1024, l)
    xl = x.reshape(bsz * l, d)
    xc = ctx.reshape(bsz * lc, d)

    rows = -(-(bsz + 1) // 8) * 8
    cvec = jnp.zeros((rows, d), F32).at[:bsz].set(c).at[bsz].set(c_ctx)
    mod = _ada_mod(cvec, ada_w, ada_b).reshape(depth, rows, 6, d)
    big = 1 << 30

    mod_l, mod_c = mod[0, :bsz], mod[0, bsz:bsz + 1]
    hl = _prenorm(xl, norm1_w[0], mod_l, tx_l, l // tx_l)
    hc = _prenorm(xc, norm1_w[0], mod_c, tb_c, big)
    tables = _s5_tables(s5_lam_re[0], s5_lam_im[0], s5_log_dt[0], s5_b_re[0], s5_b_im[0],
                        s5_c_re[0], s5_c_im[0], s5_d[0], S5_CHUNK)
    yc, yl = _s5_core(_time_major(hc, bsz, S5_CHUNK), _time_major(hl, bsz, S5_CHUNK), tables)
    w_glu = s5_w_glu[0].astype(BF16)
    w1, w2 = mlp_w1[0].astype(BF16), mlp_w2[0].astype(BF16)
    xl = _mixout(_token_major(yl), w_glu, xl, mod_l, tx_l, l // tx_l, True)
    xc = _mixout(_token_major(yc), w_glu, xc, mod_c, tm_c, big, True)
    xl = _mlp(xl, norm2_w[0], mod_l, w1, w2, final_norm_w, tm_l, l // tm_l, False)
    xc = _mlp(xc, norm2_w[0], mod_c, w1, w2, final_norm_w, tm_c, big, False)

    mod_l, mod_c = mod[1, :bsz], mod[1, bsz:bsz + 1]
    w_in = gdn_w_in[0]
    n_main = 2 * qk + 2 * vd
    w_main = w_in[:, :n_main].astype(BF16)
    w_gate = jnp.pad(w_in[:, n_main:], ((0, 0), (0, LANES - 4 * hv)))
    conv_w = gdn_conv_w[0].astype(F32)
    cw = jnp.concatenate([conv_w[:, :qk], jnp.zeros((conv_w.shape[0], vd), F32), conv_w[:, qk:]], axis=1)
    cw = jnp.pad(cw, ((0, 8 - cw.shape[0]), (0, 0)))
    neg_a = jnp.pad(-jnp.exp(gdn_a_log[0].astype(F32)).reshape(-1), (2 * hv, LANES - 4 * hv))
    dtb = jnp.pad(gdn_dt_bias[0].astype(F32).reshape(-1), (2 * hv, LANES - 4 * hv))
    gp = jnp.stack([neg_a, dtb])
    tn = min(512, qk)
    p_l, g_l, gt_l = _gdn_in(xl, norm1_w[1], mod_l, w_main, w_gate, cw, gp, tb_l, l // tb_l, GRID_W,
                             0, tn, qk, vd, hv, hdim)
    p_c, g_c, gt_c = _gdn_in(xc, norm1_w[1], mod_c, w_main, w_gate, cw, gp, tb_c, big, lc,
                             qk + vd, tn, qk, vd, hv, hdim)
    gated = _gdn_core(p_l, p_c, g_l, gt_l, g_c, gt_c, gdn_onorm_w[0].astype(F32), bsz, hk, hv, hdim)
    xl = _mixout(gated, gdn_w_out[0].astype(BF16), xl, mod_l, tx_l, l // tx_l, False)
    out = _mlp(xl, norm2_w[1], mod_l, mlp_w1[1].astype(BF16), mlp_w2[1].astype(BF16), final_norm_w,
               tm_l, l // tm_l, True)
    return out.reshape(bsz, l, d)
```

```python
import functools

import jax
import jax.numpy as jnp
import numpy as np
from jax import lax
from jax.experimental import pallas as pl
from jax.experimental.pallas import tpu as pltpu

F32 = jnp.float32
BF16 = jnp.bfloat16
HIGHEST = lax.Precision.HIGHEST

NORM_EPS = 1e-6
GRID_W = 64
GDN_CHUNK = 64
GDN_KEY_HEADS_PER_STEP = 2
GDN_PREP_UNROLL = 8
S5_CHUNK = 16
LANES = 128
VMEM_LIMIT = 56 * 1024 * 1024


def _dot(a, b, precision=None):
    return jnp.dot(a, b, preferred_element_type=F32, precision=precision)


def _dot_nt(a, b):
    return lax.dot_general(a, b, (((1,), (1,)), ((), ())), preferred_element_type=F32)


def _dot_tn(a, b):
    return lax.dot_general(a, b, (((0,), (0,)), ((), ())), preferred_element_type=F32)


def _floor_pow2(v, n):
    assert n & (n - 1) == 0
    return jnp.bitwise_and(v, -n)


def _silu(t):
    return t * jax.nn.sigmoid(t)


def _norm_mod(x, nw, shift, scale):
    y = x * lax.rsqrt(jnp.mean(x * x, axis=-1, keepdims=True) + NORM_EPS) * nw
    return y * (1.0 + scale) + shift


def _params(*sem):
    return pltpu.CompilerParams(dimension_semantics=sem, vmem_limit_bytes=VMEM_LIMIT)


def _ada_kernel(c_ref, w_ref, b_ref, o_ref):
    o_ref[0] = _dot(_silu(c_ref[...]), w_ref[0], HIGHEST) + b_ref[0]


def _ada_mod(cvec, ada_w, ada_b):
    depth, d, n = ada_w.shape
    rows = cvec.shape[0]
    tn = n // 4
    return pl.pallas_call(
        _ada_kernel,
        grid=(depth, n // tn),
        in_specs=[pl.BlockSpec((rows, d), lambda i, j: (0, 0)),
                  pl.BlockSpec((1, d, tn), lambda i, j: (i, 0, j)),
                  pl.BlockSpec((1, 1, tn), lambda i, j: (i, 0, j))],
        out_specs=pl.BlockSpec((1, rows, tn), lambda i, j: (i, 0, j)),
        out_shape=jax.ShapeDtypeStruct((depth, rows, n), F32),
        compiler_params=_params("arbitrary", "arbitrary"),
        name="ada_mod",
    )(cvec, ada_w, ada_b.reshape(depth, 1, n))


def _prenorm_kernel(x_ref, nw_ref, mod_ref, o_ref):
    h = _norm_mod(x_ref[...], nw_ref[...], mod_ref[0, 0:1, :], mod_ref[0, 1:2, :])
    o_ref[...] = h.astype(o_ref.dtype)


def _prenorm(x, nw, mod, tile, tiles_per_mod):
    n, d = x.shape
    return pl.pallas_call(
        _prenorm_kernel,
        grid=(n // tile,),
        in_specs=[pl.BlockSpec((tile, d), lambda i: (i, 0)),
                  pl.BlockSpec((1, d), lambda i: (0, 0)),
                  pl.BlockSpec((1, 6, d), lambda i: (i // tiles_per_mod, 0, 0))],
        out_specs=pl.BlockSpec((tile, d), lambda i: (i, 0)),
        out_shape=jax.ShapeDtypeStruct((n, d), BF16),
        compiler_params=_params("parallel"),
        name="prenorm",
    )(x, nw.reshape(1, d), mod)


def _s5_tables(lam_re, lam_im, log_dt, b_re, b_im, c_re, c_im, d_skip, t):
    _, g, p = lam_re.shape
    h = b_re.shape[-1]
    f = lambda a: a.astype(F32)
    lam_re, lam_im, b_re, b_im, c_re, c_im = map(f, (lam_re, lam_im, b_re, b_im, c_re, c_im))
    dt = jnp.exp(f(log_dt))[..., None]
    zr, zi = lam_re * dt, lam_im * dt
    n = jnp.arange(t + 1, dtype=F32)[:, None, None, None]
    mag = jnp.exp(n * zr)
    pr, pi = mag * jnp.cos(n * zi), mag * jnp.sin(n * zi)
    ar, ai = pr[1], pi[1]
    den = lam_re * lam_re + lam_im * lam_im
    fr = ((ar - 1.0) * lam_re + ai * lam_im) / den
    fi = (ai * lam_re - (ar - 1.0) * lam_im) / den
    bbr = fr[..., None] * b_re - fi[..., None] * b_im
    bbi = fr[..., None] * b_im + fi[..., None] * b_re
    prt, pit = jnp.transpose(pr, (1, 2, 0, 3))[:, :, :, None, :], jnp.transpose(pi, (1, 2, 0, 3))[:, :, :, None, :]
    car = c_re[:, :, None] * prt - c_im[:, :, None] * pit
    cai = c_re[:, :, None] * pit + c_im[:, :, None] * prt
    kern = jnp.einsum('dgmq,dgqi->dgmi', jnp.concatenate([car, -cai], axis=-1).reshape(2, g, (t + 1) * h, 2 * p),
                      jnp.concatenate([bbr, bbi], axis=2), precision=HIGHEST).reshape(2, g, t + 1, h, h)
    ti = jnp.arange(t)
    lag = np.arange(t)[None, :] - np.arange(t)[:, None]
    place = np.stack([(lag == n) for n in range(t + 1)] + [(-lag == n) for n in range(t + 1)])
    place = place.reshape(2, t + 1, t, t).astype(np.float32)
    intra = jnp.einsum('dnst,dgnoi->gsito', place, kern, precision=HIGHEST).reshape(g, t * h, t * h)

    def inject(d, powers):
        er, ei = pr[powers, d], pi[powers, d]
        re = er[..., None] * bbr[d][None] - ei[..., None] * bbi[d][None]
        im = er[..., None] * bbi[d][None] + ei[..., None] * bbr[d][None]
        tr = lambda a: jnp.transpose(a, (1, 0, 3, 2)).reshape(g, t * h, p)
        return tr(re), tr(im)

    pad = lambda a: jnp.pad(a, ((0, 0), (0, 0), (0, LANES - p)))
    in_f = inject(0, t - 1 - ti)
    in_b = inject(1, ti)
    w1 = jnp.concatenate([intra] + [pad(a) for a in (*in_f, *in_b)], axis=-1)

    def readout(d, powers):
        tr = lambda a: jnp.transpose(a, (0, 3, 1, 2)).reshape(g, p, t * h)
        return tr(car[d][:, powers]), tr(-cai[d][:, powers])

    padr = lambda a: jnp.pad(a, ((0, 0), (0, LANES - p), (0, 0)))
    w2 = jnp.concatenate([padr(a) for a in (*readout(0, ti + 1), *readout(1, t - ti))], axis=1)
    dec = jnp.stack([pr[t, 0], pi[t, 0], pr[t, 1], pi[t, 1]], axis=1)
    dec = jnp.pad(dec, ((0, 0), (0, 4), (0, LANES - p)))
    dsk = jnp.tile(f(d_skip).reshape(g, 1, h), (1, t, 1)).reshape(g, 1, t * h)
    return w1.astype(BF16), w2.astype(BF16), dec, dsk


def _s5_kernel(hc_ref, hl_ref, pg_ref, pt_ref, w1_ref, w2_ref, dec_ref, dsk_ref, yc_ref, yl_ref,
               z_scr, h_scr, u_scr, *, bsz, ncc, ncl, width, row_blk):
    rc, rl = ncc * bsz, ncl * bsz
    t = hl_ref.shape[0]
    cb = max(1, row_blk // bsz)
    w1 = w1_ref[0]
    pg = pg_ref[0]

    @pl.when(pl.program_id(1) == 0)
    def _():
        yc_ref[...] = jnp.zeros(yc_ref.shape, BF16)
        yl_ref[...] = jnp.zeros(yl_ref.shape, BF16)

    def blocks():
        for h_ref, y_ref, base, nch in ((hc_ref, yc_ref, 0, ncc), (hl_ref, yl_ref, rc, ncl)):
            for c0 in range(0, nch, cb):
                n = min(cb, nch - c0)
                yield h_ref, y_ref, c0, n, slice(base + c0 * bsz, base + (c0 + n) * bsz)

    for h_ref, _, c0, n, sl in blocks():
        xcat = jnp.concatenate([h_ref[tt, c0:c0 + n].reshape(n * bsz, LANES) for tt in range(t)], axis=1)
        u = _dot(xcat, pg).astype(BF16)
        u_scr[sl, :] = u
        z_scr[sl, :] = _dot(u, w1)
    dec = dec_ref[0]
    zero = jnp.zeros((bsz, LANES), F32)
    cols = [width + k * LANES for k in range(5)]

    def make_step(are, aim, col, hcol):
        def step(row, carry):
            re, im = carry
            r = pl.multiple_of(row, bsz)
            h_scr[pl.ds(r, bsz), hcol:hcol + LANES] = re
            h_scr[pl.ds(r, bsz), hcol + LANES:hcol + 2 * LANES] = im
            s_re = z_scr[pl.ds(r, bsz), cols[col]:cols[col + 1]]
            s_im = z_scr[pl.ds(r, bsz), cols[col + 1]:cols[col + 2]]
            return are * re - aim * im + s_re, are * im + aim * re + s_im
        return step

    fstep = make_step(dec[0:1, :], dec[1:2, :], 0, 0)
    bstep = make_step(dec[2:3, :], dec[3:4, :], 2, 2 * LANES)
    lax.fori_loop(0, ncc + ncl, lambda c, s: fstep(c * bsz, s), (zero, zero))
    st = lax.fori_loop(0, ncc, lambda i, s: bstep((ncc - 1 - i) * bsz, s), (zero, zero))
    lax.fori_loop(0, ncl, lambda i, s: bstep((ncc + ncl - 1 - i) * bsz, s), st)

    w2 = w2_ref[0]
    dsk = dsk_ref[0]
    pt = pt_ref[0]
    for _, y_ref, c0, n, sl in blocks():
        y = z_scr[sl, 0:width] + _dot(h_scr[sl, :].astype(BF16), w2)
        y = jax.nn.gelu(y + dsk * u_scr[sl, :].astype(F32)).astype(BF16)
        yn = _dot(y, pt)
        for tt in range(t):
            slab = yn[:, tt * LANES:(tt + 1) * LANES].astype(BF16).reshape(n, bsz, LANES)
            y_ref[tt, c0:c0 + n] = y_ref[tt, c0:c0 + n] + slab


def _s5_lane_perm(t, hch):
    ng = LANES // hch
    pg = np.zeros((ng, t, LANES, t, hch), np.float32)
    for g in range(ng):
        for tt in range(t):
            pg[g, tt, g * hch + np.arange(hch), tt, np.arange(hch)] = 1.0
    return pg.reshape(ng, t * LANES, t * hch)


def _s5_core(hc, hl, tables, row_blk=512):
    w1, w2, dec, dsk = tables
    t, ncc, bsz, d = hc.shape
    ncl = hl.shape[1]
    width = w2.shape[2]
    hch = width // t
    ng = LANES // hch
    rows = (ncc + ncl) * bsz
    pg = _s5_lane_perm(t, hch)
    kern = functools.partial(_s5_kernel, bsz=bsz, ncc=ncc, ncl=ncl, width=width, row_blk=row_blk)
    gmap = lambda j, q: (j * ng + q, 0, 0)
    qmap = lambda j, q: (q, 0, 0)
    once = pl.Buffered(1)
    return pl.pallas_call(
        kern,
        grid=(d // LANES, ng),
        in_specs=[pl.BlockSpec((t, ncc, bsz, LANES), lambda j, q: (0, 0, 0, j), pipeline_mode=once),
                  pl.BlockSpec((t, ncl, bsz, LANES), lambda j, q: (0, 0, 0, j), pipeline_mode=once),
                  pl.BlockSpec((1,) + pg.shape[1:], qmap), pl.BlockSpec((1, t * hch, t * LANES), qmap),
                  pl.BlockSpec((1,) + w1.shape[1:], gmap), pl.BlockSpec((1,) + w2.shape[1:], gmap),
                  pl.BlockSpec((1,) + dec.shape[1:], gmap), pl.BlockSpec((1,) + dsk.shape[1:], gmap)],
        out_specs=[pl.BlockSpec((t, ncc, bsz, LANES), lambda j, q: (0, 0, 0, j), pipeline_mode=once),
                   pl.BlockSpec((t, ncl, bsz, LANES), lambda j, q: (0, 0, 0, j), pipeline_mode=once)],
        out_shape=[jax.ShapeDtypeStruct(hc.shape, BF16), jax.ShapeDtypeStruct(hl.shape, BF16)],
        scratch_shapes=[pltpu.VMEM((rows, w1.shape[2]), F32), pltpu.VMEM((rows, 4 * LANES), F32),
                        pltpu.VMEM((rows, width), BF16)],
        compiler_params=_params("parallel", "arbitrary"),
        name="s5_core",
    )(hc, hl, jnp.asarray(pg, dtype=BF16), jnp.asarray(pg.transpose(0, 2, 1), dtype=BF16), w1, w2, dec, dsk)


def _time_major(hm, bsz, t):
    n, d = hm.shape
    return jnp.transpose(hm.reshape(bsz, n // bsz // t, t, d), (2, 1, 0, 3))


def _token_major(y):
    t, nc, bsz, d = y.shape
    return jnp.transpose(y, (2, 1, 0, 3)).reshape(bsz * nc * t, d)


def _mixout_kernel(y_ref, w_ref, x_ref, mod_ref, o_ref, *, glu):
    r = _dot(y_ref[...], w_ref[...])
    if glu:
        half = r.shape[-1] // 2
        r = r[:, :half] * jax.nn.sigmoid(r[:, half:])
    o_ref[...] = x_ref[...] + mod_ref[0, 2:3, :] * r


def _mixout(y, w, x, mod, tile, tiles_per_mod, glu):
    n, d = x.shape
    k, nn = w.shape
    return pl.pallas_call(
        functools.partial(_mixout_kernel, glu=glu),
        grid=(n // tile,),
        in_specs=[pl.BlockSpec((tile, k), lambda i: (i, 0)),
                  pl.BlockSpec((k, nn), lambda i: (0, 0)),
                  pl.BlockSpec((tile, d), lambda i: (i, 0)),
                  pl.BlockSpec((1, 6, d), lambda i: (i // tiles_per_mod, 0, 0))],
        out_specs=pl.BlockSpec((tile, d), lambda i: (i, 0)),
        out_shape=jax.ShapeDtypeStruct((n, d), F32),
        compiler_params=_params("parallel"),
        name="mixer_out",
    )(y, w, x, mod)


def _mlp_kernel(x_ref, nw_ref, mod_ref, w1_ref, w2_ref, fw_ref, o_ref, *, ff_blk, final):
    x = x_ref[...]
    h = _norm_mod(x, nw_ref[...], mod_ref[0, 3:4, :], mod_ref[0, 4:5, :]).astype(BF16)
    acc = jnp.zeros(x.shape, F32)
    for f0 in range(0, w1_ref.shape[1], ff_blk):
        a = jnp.maximum(_dot(h, w1_ref[:, f0:f0 + ff_blk]), 0.0)
        acc = acc + _dot((a * a).astype(BF16), w2_ref[f0:f0 + ff_blk, :])
    y = x + mod_ref[0, 5:6, :] * acc
    if final:
        y = y * lax.rsqrt(jnp.mean(y * y, axis=-1, keepdims=True) + NORM_EPS) * fw_ref[...]
    o_ref[...] = y


def _mlp(x, nw, mod, w1, w2, fw, tile, tiles_per_mod, final):
    n, d = x.shape
    ff = w1.shape[1]
    return pl.pallas_call(
        functools.partial(_mlp_kernel, ff_blk=min(1024, ff), final=final),
        grid=(n // tile,),
        in_specs=[pl.BlockSpec((tile, d), lambda i: (i, 0)),
                  pl.BlockSpec((1, d), lambda i: (0, 0)),
                  pl.BlockSpec((1, 6, d), lambda i: (i // tiles_per_mod, 0, 0)),
                  pl.BlockSpec((d, ff), lambda i: (0, 0)),
                  pl.BlockSpec((ff, d), lambda i: (0, 0)),
                  pl.BlockSpec((1, d), lambda i: (0, 0))],
        out_specs=pl.BlockSpec((tile, d), lambda i: (i, 0)),
        out_shape=jax.ShapeDtypeStruct((n, d), F32),
        compiler_params=_params("parallel"),
        name="mlp",
    )(x, nw.reshape(1, d), mod, w1, w2, fw.reshape(1, d))


def _gdn_in_kernel(x_ref, nw_ref, mod_ref, w_ref, wg_ref, cw_ref, gp_ref, p_ref, g_ref, gt_ref, acc_scr,
                   *, period, col0, qk, vd, tn, hv, hdim):
    tb = x_ref.shape[0]
    h = _norm_mod(x_ref[...], nw_ref[...], mod_ref[0, 0:1, :], mod_ref[0, 1:2, :])
    hb = h.astype(BF16)
    gl = _dot(h, wg_ref[...], HIGHEST)
    lane = lax.broadcasted_iota(jnp.int32, (1, LANES), 1)
    beta = jax.nn.sigmoid(gl)
    t = gl + gp_ref[1:2, :]
    g = gp_ref[0:1, :] * (jnp.maximum(t, 0.0) + jnp.log1p(jnp.exp(-jnp.abs(t))))
    r = lax.broadcasted_iota(jnp.int32, (tb, tb), 0)
    c = lax.broadcasted_iota(jnp.int32, (tb, tb), 1)
    same = _floor_pow2(r, GDN_CHUNK) == _floor_pow2(c, GDN_CHUNK)
    gcf = _dot(jnp.where(same & (c <= r), 1.0, 0.0), g, HIGHEST)
    gcb = _dot(jnp.where(same & (c >= r), 1.0, 0.0), g, HIGHEST)
    out = jnp.where(lane < 2 * hv, beta, jnp.where(lane < 3 * hv, gcf, jnp.where(lane < 4 * hv, gcb, 0.0)))
    g_ref[...] = out
    for cc in range(tb // GDN_CHUNK):
        gt_ref[cc] = out[cc * GDN_CHUNK:(cc + 1) * GDN_CHUNK, :].T[2 * hv:4 * hv, :]

    assert tb % period == 0
    row = lax.broadcasted_iota(jnp.int32, (period, 1), 0)
    edge = [(tap, ((row + tap - 2 >= 0) & (row + tap - 2 < period)).astype(F32)) for tap in (0, 1, 3, 4)]

    for bi, o0 in enumerate(range(0, p_ref.shape[1], tn)):
        c0 = col0 + o0
        acc = _dot(hb, w_ref[:, c0:c0 + tn])
        if qk <= c0 < qk + vd:
            p_ref[:, o0:o0 + tn] = acc.astype(BF16)
            continue
        acc_ref = acc_scr.at[bi % 2]
        acc_ref[...] = acc
        scale = hdim ** -0.5 if c0 < qk else (1.0 if c0 < 2 * qk + vd else None)
        for h0 in range(0, tn, hdim):
            cw = cw_ref[:, c0 + h0:c0 + h0 + hdim]
            taps = [(tap, ok * cw[tap:tap + 1, :]) for tap, ok in edge]
            for r0 in range(0, tb, period):
                a = acc_ref[r0:r0 + period, h0:h0 + hdim]
                y = a * cw[2:3, :]
                for tap, wt in taps:
                    y = y + pltpu.roll(a, (period - (tap - 2)) % period, 0) * wt
                y = _silu(y)
                if scale is not None:
                    y = y * (lax.rsqrt(jnp.sum(y * y, axis=-1, keepdims=True) + NORM_EPS) * scale)
                p_ref[r0:r0 + period, o0 + h0:o0 + h0 + hdim] = y.astype(BF16)


def _gdn_in(x, nw, mod, w_main, w_gate, cw, gp, tile, tiles_per_mod, period, col0, tn, qk, vd, hv, hdim):
    n, d = x.shape
    n_main = w_main.shape[1]
    kern = functools.partial(_gdn_in_kernel, period=period, col0=col0, qk=qk, vd=vd, tn=tn, hv=hv, hdim=hdim)
    return pl.pallas_call(
        kern,
        grid=(n // tile,),
        in_specs=[pl.BlockSpec((tile, d), lambda i: (i, 0)),
                  pl.BlockSpec((1, d), lambda i: (0, 0)),
                  pl.BlockSpec((1, 6, d), lambda i: (i // tiles_per_mod, 0, 0)),
                  pl.BlockSpec((d, n_main), lambda i: (0, 0)),
                  pl.BlockSpec((d, LANES), lambda i: (0, 0)),
                  pl.BlockSpec((8, n_main), lambda i: (0, 0)),
                  pl.BlockSpec((2, LANES), lambda i: (0, 0))],
        out_specs=[pl.BlockSpec((tile, n_main - col0), lambda i: (i, 0)),
                   pl.BlockSpec((tile, LANES), lambda i: (i, 0)),
                   pl.BlockSpec((tile // GDN_CHUNK, 2 * hv, GDN_CHUNK), lambda i: (i, 0, 0))],
        out_shape=[jax.ShapeDtypeStruct((n, n_main - col0), BF16),
                   jax.ShapeDtypeStruct((n, LANES), F32),
                   jax.ShapeDtypeStruct((n // GDN_CHUNK, 2 * hv, GDN_CHUNK), F32)],
        scratch_shapes=[pltpu.VMEM((2, tile, tn), F32)],
        compiler_params=_params("parallel"),
        name="gdn_in",
    )(x, nw.reshape(1, d), mod, w_main, w_gate, cw, gp)


def _gdn_masks(ch, rep):
    nco = 2 * rep
    r, c = np.meshgrid(np.arange(ch), np.arange(ch), indexing="ij")
    per_dir = []
    for lower in (True, False):
        m = [(c <= r) if lower else (c >= r), (c < r) if lower else (c > r)]
        b = 1
        while b < ch:
            same = (r // (2 * b)) == (c // (2 * b))
            hi_r, hi_c = (r % (2 * b)) >= b, (c % (2 * b)) >= b
            m.append(same & ((hi_r & ~hi_c) if lower else (~hi_r & hi_c)))
            b *= 2
        per_dir.append(np.stack(m))
    cm = np.concatenate([per_dir[j // rep] for j in range(nco)], axis=2).astype(np.float32)
    blk = np.arange(nco * ch) // ch
    return cm, (blk[:, None] == blk[None, :]).astype(np.float32)


def _gdn_kernel(q_ref, k_ref, v_ref, z_ref, kc_ref, vc_ref, g_ref, gt_ref, gc_ref, gct_ref, ow_ref, cm_ref, lv_ref, bd_ref,
                o_ref, of_s, ob_s, st_s, uw_s, in_s, kt_s, *, hv, khb, rep, hdim, ncc, ncl, unroll):
    ch = GDN_CHUNK
    nco = 2 * rep
    nlvl = lv_ref.shape[0]
    kh0 = pl.program_id(1) * khb
    lane = lax.broadcasted_iota(jnp.int32, (1, LANES), 1)
    lcat = lax.broadcasted_iota(jnp.int32, (1, nco * ch), 1)

    def lanes_of(khl, j):
        head = (kh0 + khl) * rep + j % rep
        return (j // rep) * hv + head, (2 + j // rep) * hv + head

    def column(gt, idx):
        return jnp.sum(jnp.where(lane == idx, gt, 0.0), axis=1, keepdims=True)

    def spread(cols):
        out = jnp.broadcast_to(cols[nco - 1], (ch, nco * ch))
        for j in range(nco - 2, -1, -1):
            out = jnp.where(lcat < (j + 1) * ch, cols[j], out)
        return out

    def block_diag(xb):
        return jnp.concatenate([xb] * nco, axis=0) * bd_ref[...]

    def prepare(insts):
        st = []
        for khl, k_r, q_r, v_r, g_r, gt_r, c, slot in insts:
            rows = pl.ds(pl.multiple_of(c * ch, ch), ch)
            kb = k_r[rows, khl * hdim:(khl + 1) * hdim]
            kcat = jnp.concatenate([kb] * nco, axis=0)
            if q_r is None:
                kkc, qkc = _dot_nt(kb, kcat), None
            else:
                prod = _dot_nt(jnp.concatenate([kb, q_r[rows, khl * hdim:(khl + 1) * hdim]], axis=0), kcat)
                kkc, qkc = prod[:ch], prod[ch:]
            gtile = g_r[rows, :]
            betas, gcols, grows = [], [], []
            for j in range(nco):
                bl, gl = lanes_of(khl, j)
                betas.append(column(gtile, bl))
                gcols.append(column(gtile, gl))
                grows.append(gt_r[c, pl.ds(gl - 2 * hv, 1), :])
            dlog = spread(gcols) - jnp.concatenate(grows, axis=1)
            decay = jnp.exp(jnp.where(cm_ref[0] > 0.0, dlog, -jnp.inf))
            ac = spread(betas) * kkc * decay * cm_ref[1]
            xb = ((cm_ref[0] - cm_ref[1]) - ac * cm_ref[2]).astype(BF16)
            st.append([rows, kb, qkc, betas, gcols, decay, ac.astype(BF16), xb])
        for lvl in range(1, nlvl):
            ps = [_dot(s[6] * lv_ref[lvl], block_diag(s[7])) for s in st]
            for s, p in zip(st, ps):
                s[7] = s[7] - _dot(s[7], block_diag(p.astype(BF16))).astype(BF16)
        outs = []
        for (khl, k_r, q_r, v_r, g_r, gt_r, c, slot), (rows, kb, qkc, betas, gcols, decay, ac, xc) in zip(insts, st):
            kf = kb.astype(F32)
            rst = []
            for j in range(nco):
                v0 = (khl * rep + j % rep) * hdim
                vf = v_r[rows, v0:v0 + hdim].astype(F32)
                rst.append(jnp.concatenate([vf * betas[j], kf * (betas[j] * jnp.exp(gcols[j]))], axis=1))
            uw = _dot(block_diag(xc), jnp.concatenate(rst, axis=0).astype(BF16))
            intra = None if qkc is None else (qkc * decay * cm_ref[0]).astype(BF16)
            outs.append((uw, kf.T.astype(BF16), intra))
        for (khl, k_r, q_r, v_r, g_r, gt_r, c, slot), (uw, kt, intra) in zip(insts, outs):
            for j in range(nco):
                uw_s[slot, khl * nco + j] = uw[j * ch:(j + 1) * ch, :].astype(BF16)
                if intra is not None:
                    in_s[c, khl * nco + j] = intra[:, j * ch:(j + 1) * ch]
            kt_s[slot, khl] = kt

    def advance(items):
        mid = []
        for khl, j, q_r, g_r, gt_r, c, slot in items:
            rows = pl.ds(pl.multiple_of(c * ch, ch), ch)
            _, gl = lanes_of(khl, j)
            gcol = column(g_r[rows, :], gl)
            grow = gt_r[c, pl.ds(gl - 2 * hv, 1), :]
            glast = grow[:, ch - 1:ch] if j < rep else grow[:, 0:1]
            sidx = khl * nco + j
            s = st_s[sidx]
            lhs = uw_s[slot, sidx, :, hdim:]
            if q_r is not None:
                qg = q_r[rows, khl * hdim:(khl + 1) * hdim].astype(F32) * jnp.exp(gcol)
                lhs = jnp.concatenate([lhs, qg.astype(BF16)], axis=0)
            mid.append((rows, gcol, glast, sidx, s, _dot(lhs, s.astype(BF16))))
        res = []
        for (khl, j, q_r, g_r, gt_r, c, slot), (rows, gcol, glast, sidx, s, ws) in zip(items, mid):
            vnew = uw_s[slot, sidx, :, :hdim].astype(F32) - ws[:ch]
            snew = s * jnp.exp(glast) + _dot(kt_s[slot, khl], (vnew * jnp.exp(glast - gcol)).astype(BF16))
            o = None if q_r is None else ws[ch:] + _dot(in_s[c, sidx], vnew.astype(BF16))
            res.append((snew, o))
        for (khl, j, q_r, g_r, gt_r, c, slot), (rows, gcol, glast, sidx, s, ws), (snew, o) in zip(items, mid, res):
            st_s[sidx] = snew
            if o is not None:
                v0 = (khl * rep + j % rep) * hdim
                (of_s if j < rep else ob_s)[rows, v0:v0 + hdim] = o.astype(BF16)

    st_s[...] = jnp.zeros(st_s.shape, F32)
    uc = min(unroll, ncc)

    def prep_ctx(i, carry):
        prepare([(khl, kc_ref, None, vc_ref, gc_ref, gct_ref, i * uc + t, i * uc + t)
                 for t in range(uc) for khl in range(khb)])
        return carry

    def prep_lat(i, carry):
        prepare([(khl, k_ref, q_ref, v_ref, g_ref, gt_ref, i * unroll + t, ncc + i * unroll + t)
                 for t in range(unroll) for khl in range(khb)])
        return carry

    def adv_ctx(i, carry):
        advance([(khl, j, None, gc_ref, gct_ref, i if j < rep else ncc - 1 - i, i if j < rep else ncc - 1 - i)
                 for khl in range(khb) for j in range(nco)])
        return carry

    def adv_lat(i, carry):
        advance([(khl, j, q_ref, g_ref, gt_ref, i if j < rep else ncl - 1 - i,
                  ncc + (i if j < rep else ncl - 1 - i)) for khl in range(khb) for j in range(nco)])
        return carry

    lax.fori_loop(0, ncc // uc, prep_ctx, 0)
    lax.fori_loop(0, ncl // unroll, prep_lat, 0)
    lax.fori_loop(0, ncc, adv_ctx, 0)
    lax.fori_loop(0, ncl, adv_lat, 0)

    blk = min(256, ncl * ch)
    ow = ow_ref[...]
    for r0 in range(0, ncl * ch, blk):
        for v0 in range(0, khb * rep * hdim, hdim):
            o = of_s[r0:r0 + blk, v0:v0 + hdim].astype(F32) + ob_s[r0:r0 + blk, v0:v0 + hdim].astype(F32)
            o = o * lax.rsqrt(jnp.mean(o * o, axis=-1, keepdims=True) + NORM_EPS) * ow
            o_ref[r0:r0 + blk, v0:v0 + hdim] = (o * _silu(z_ref[r0:r0 + blk, v0:v0 + hdim].astype(F32))).astype(BF16)


def _gdn_core(p_lat, p_ctx, g_lat, gt_lat, g_ctx, gt_ctx, onorm_w, bsz, hk, hv, hdim, khb=GDN_KEY_HEADS_PER_STEP):
    l = p_lat.shape[0] // bsz
    lc = p_ctx.shape[0] // bsz
    ch = GDN_CHUNK
    ncc, ncl = lc // ch, l // ch
    rep = hv // hk
    nco = 2 * rep
    kw, vw = khb * hdim, khb * rep * hdim
    qk, vd = hk * hdim, hv * hdim
    unroll = min(GDN_PREP_UNROLL, ncl)
    assert hk % khb == 0 and ncl % unroll == 0 and ncc % min(unroll, ncc) == 0
    assert qk % vw == 0 and vd % vw == 0
    kern = functools.partial(_gdn_kernel, hv=hv, khb=khb, rep=rep, hdim=hdim, ncc=ncc, ncl=ncl, unroll=unroll)
    kcol, vcol, zcol = (qk + vd) // kw, (2 * qk + vd) // vw, qk // vw
    cm, bd = _gdn_masks(ch, rep)
    return pl.pallas_call(
        kern,
        grid=(bsz, hk // khb),
        in_specs=[pl.BlockSpec((l, kw), lambda b, h: (b, h)),
                  pl.BlockSpec((l, kw), lambda b, h: (b, kcol + h)),
                  pl.BlockSpec((l, vw), lambda b, h: (b, vcol + h)),
                  pl.BlockSpec((l, vw), lambda b, h: (b, zcol + h)),
                  pl.BlockSpec((lc, kw), lambda b, h: (b, h)),
                  pl.BlockSpec((lc, vw), lambda b, h: (b, qk // vw + h)),
                  pl.BlockSpec((l, LANES), lambda b, h: (b, 0)),
                  pl.BlockSpec((ncl, 2 * hv, ch), lambda b, h: (b, 0, 0)),
                  pl.BlockSpec((lc, LANES), lambda b, h: (b, 0)),
                  pl.BlockSpec((ncc, 2 * hv, ch), lambda b, h: (b, 0, 0)),
                  pl.BlockSpec((1, hdim), lambda b, h: (0, 0)),
                  pl.BlockSpec(cm[:3].shape, lambda b, h: (0, 0, 0)),
                  pl.BlockSpec(cm[2:].shape, lambda b, h: (0, 0, 0)),
                  pl.BlockSpec(bd.shape, lambda b, h: (0, 0))],
        out_specs=pl.BlockSpec((l, vw), lambda b, h: (b, h)),
        out_shape=jax.ShapeDtypeStruct((bsz * l, vd), BF16),
        scratch_shapes=[pltpu.VMEM((l, vw), BF16), pltpu.VMEM((l, vw), BF16),
                        pltpu.VMEM((khb * nco, hdim, hdim), F32),
                        pltpu.VMEM((ncc + ncl, khb * nco, ch, 2 * hdim), BF16),
                        pltpu.VMEM((ncl, khb * nco, ch, ch), BF16),
                        pltpu.VMEM((ncc + ncl, khb, hdim, ch), BF16)],
        compiler_params=_params("parallel", "arbitrary"),
        name="gdn_core",
    )(p_lat, p_lat, p_lat, p_lat, p_ctx, p_ctx, g_lat, gt_lat, g_ctx, gt_ctx, onorm_w.reshape(1, hdim),
      jnp.asarray(cm[:3]), jnp.asarray(cm[2:], dtype=BF16), jnp.asarray(bd, dtype=BF16))


def kernel(x, c, ctx, c_ctx, ada_w, ada_b, norm1_w, norm2_w, mlp_w1, mlp_w2, s5_lam_re, s5_lam_im, s5_log_dt, s5_b_re, s5_b_im, s5_c_re, s5_c_im, s5_d, s5_w_glu, gdn_w_in, gdn_conv_w, gdn_a_log, gdn_dt_bias, gdn_onorm_w, gdn_w_out, final_norm_w):
    bsz, l, d = x.shape
    lc = ctx.shape[1]
    depth = ada_w.shape[0]
    assert depth == 2 and s5_lam_re.shape[0] == 1 and gdn_w_in.shape[0] == 1
    hv = gdn_a_log.shape[2]
    hdim = gdn_onorm_w.shape[1]
    vd = gdn_w_out.shape[1]
    qk = (gdn_conv_w.shape[2] - vd) // 2
    hk = qk // hdim
    assert l % GRID_W == 0 and l % GDN_CHUNK == 0 and lc % GDN_CHUNK == 0
    assert l % S5_CHUNK == 0 and lc % S5_CHUNK == 0 and 4 * hv <= LANES

    tb_l, tb_c = min(256, l), lc
    tm_l, tm_c = min(512, l), min(512, bsz * lc)
    tx_l = min(1024, l)
    xl = x.reshape(bsz * l, d)
    xc = ctx.reshape(bsz * lc, d)

    rows = -(-(bsz + 1) // 8) * 8
    cvec = jnp.zeros((rows, d), F32).at[:bsz].set(c).at[bsz].set(c_ctx)
    mod = _ada_mod(cvec, ada_w, ada_b).reshape(depth, rows, 6, d)
    big = 1 << 30

    mod_l, mod_c = mod[0, :bsz], mod[0, bsz:bsz + 1]
    hl = _prenorm(xl, norm1_w[0], mod_l, tx_l, l // tx_l)
    hc = _prenorm(xc, norm1_w[0], mod_c, tb_c, big)
    tables = _s5_tables(s5_lam_re[0], s5_lam_im[0], s5_log_dt[0], s5_b_re[0], s5_b_im[0],
                        s5_c_re[0], s5_c_im[0], s5_d[0], S5_CHUNK)
    yc, yl = _s5_core(_time_major(hc, bsz, S5_CHUNK), _time_major(hl, bsz, S5_CHUNK), tables)
    w_glu = s5_w_glu[0].astype(BF16)
    w1, w2 = mlp_w1[0].astype(BF16), mlp_w2[0].astype(BF16)
    xl = _mixout(_token_major(yl), w_glu, xl, mod_l, tx_l, l // tx_l, True)
    xc = _mixout(_token_major(yc), w_glu, xc, mod_c, tm_c, big, True)
    xl = _mlp(xl, norm2_w[0], mod_l, w1, w2, final_norm_w, tm_l, l // tm_l, False)
    xc = _mlp(xc, norm2_w[0], mod_c, w1, w2, final_norm_w, tm_c, big, False)

    mod_l, mod_c = mod[1, :bsz], mod[1, bsz:bsz + 1]
    w_in = gdn_w_in[0]
    n_main = 2 * qk + 2 * vd
    w_main = w_in[:, :n_main].astype(BF16)
    w_gate = jnp.pad(w_in[:, n_main:], ((0, 0), (0, LANES - 4 * hv)))
    conv_w = gdn_conv_w[0].astype(F32)
    cw = jnp.concatenate([conv_w[:, :qk], jnp.zeros((conv_w.shape[0], vd), F32), conv_w[:, qk:]], axis=1)
    cw = jnp.pad(cw, ((0, 8 - cw.shape[0]), (0, 0)))
    neg_a = jnp.pad(-jnp.exp(gdn_a_log[0].astype(F32)).reshape(-1), (2 * hv, LANES - 4 * hv))
    dtb = jnp.pad(gdn_dt_bias[0].astype(F32).reshape(-1), (2 * hv, LANES - 4 * hv))
    gp = jnp.stack([neg_a, dtb])
    tn = min(512, qk)
    p_l, g_l, gt_l = _gdn_in(xl, norm1_w[1], mod_l, w_main, w_gate, cw, gp, tb_l, l // tb_l, GRID_W,
                             0, tn, qk, vd, hv, hdim)
    p_c, g_c, gt_c = _gdn_in(xc, norm1_w[1], mod_c, w_main, w_gate, cw, gp, tb_c, big, lc,
                             qk + vd, tn, qk, vd, hv, hdim)
    gated = _gdn_core(p_l, p_c, g_l, gt_l, g_c, gt_c, gdn_onorm_w[0].astype(F32), bsz, hk, hv, hdim)
    xl = _mixout(gated, gdn_w_out[0].astype(BF16), xl, mod_l, tx_l, l // tx_l, False)
    out = _mlp(xl, norm2_w[1], mod_l, mlp_w1[1].astype(BF16), mlp_w2[1].astype(BF16), final_norm_w,
               tm_l, l // tm_l, True)
    return out.reshape(bsz, l, d)
```

```python
import functools

import jax
import jax.numpy as jnp
import numpy as np
from jax import lax
from jax.experimental import pallas as pl
from jax.experimental.pallas import tpu as pltpu

F32 = jnp.float32
BF16 = jnp.bfloat16
HIGHEST = lax.Precision.HIGHEST

NORM_EPS = 1e-6
GRID_W = 64
GDN_CHUNK = 64
GDN_KEY_HEADS_PER_STEP = 2
GDN_PREP_UNROLL = 8
S5_CHUNK = 16
LANES = 128
VMEM_LIMIT = 56 * 1024 * 1024


def _dot(a, b, precision=None):
    return jnp.dot(a, b, preferred_element_type=F32, precision=precision)


def _dot_nt(a, b):
    return lax.dot_general(a, b, (((1,), (1,)), ((), ())), preferred_element_type=F32)


def _dot_tn(a, b):
    return lax.dot_general(a, b, (((0,), (0,)), ((), ())), preferred_element_type=F32)


def _floor_pow2(v, n):
    assert n & (n - 1) == 0
    return jnp.bitwise_and(v, -n)


def _silu(t):
    return t * jax.nn.sigmoid(t)


def _norm_mod(x, nw, shift, scale):
    y = x * lax.rsqrt(jnp.mean(x * x, axis=-1, keepdims=True) + NORM_EPS) * nw
    return y * (1.0 + scale) + shift


def _params(*sem):
    return pltpu.CompilerParams(dimension_semantics=sem, vmem_limit_bytes=VMEM_LIMIT)


def _ada_kernel(c_ref, w_ref, b_ref, o_ref):
    o_ref[0] = _dot(_silu(c_ref[...]), w_ref[0], HIGHEST) + b_ref[0]


def _ada_mod(cvec, ada_w, ada_b):
    depth, d, n = ada_w.shape
    rows = cvec.shape[0]
    tn = n // 4
    return pl.pallas_call(
        _ada_kernel,
        grid=(depth, n // tn),
        in_specs=[pl.BlockSpec((rows, d), lambda i, j: (0, 0)),
                  pl.BlockSpec((1, d, tn), lambda i, j: (i, 0, j)),
                  pl.BlockSpec((1, 1, tn), lambda i, j: (i, 0, j))],
        out_specs=pl.BlockSpec((1, rows, tn), lambda i, j: (i, 0, j)),
        out_shape=jax.ShapeDtypeStruct((depth, rows, n), F32),
        compiler_params=_params("arbitrary", "arbitrary"),
        name="ada_mod",
    )(cvec, ada_w, ada_b.reshape(depth, 1, n))


def _prenorm_kernel(x_ref, nw_ref, mod_ref, o_ref):
    h = _norm_mod(x_ref[...], nw_ref[...], mod_ref[0, 0:1, :], mod_ref[0, 1:2, :])
    o_ref[...] = h.astype(o_ref.dtype)


def _prenorm(x, nw, mod, tile, tiles_per_mod):
    n, d = x.shape
    return pl.pallas_call(
        _prenorm_kernel,
        grid=(n // tile,),
        in_specs=[pl.BlockSpec((tile, d), lambda i: (i, 0)),
                  pl.BlockSpec((1, d), lambda i: (0, 0)),
                  pl.BlockSpec((1, 6, d), lambda i: (i // tiles_per_mod, 0, 0))],
        out_specs=pl.BlockSpec((tile, d), lambda i: (i, 0)),
        out_shape=jax.ShapeDtypeStruct((n, d), BF16),
        compiler_params=_params("parallel"),
        name="prenorm",
    )(x, nw.reshape(1, d), mod)


def _s5_tables(lam_re, lam_im, log_dt, b_re, b_im, c_re, c_im, d_skip, t):
    _, g, p = lam_re.shape
    h = b_re.shape[-1]
    f = lambda a: a.astype(F32)
    lam_re, lam_im, b_re, b_im, c_re, c_im = map(f, (lam_re, lam_im, b_re, b_im, c_re, c_im))
    dt = jnp.exp(f(log_dt))[..., None]
    zr, zi = lam_re * dt, lam_im * dt
    n = jnp.arange(t + 1, dtype=F32)[:, None, None, None]
    mag = jnp.exp(n * zr)
    pr, pi = mag * jnp.cos(n * zi), mag * jnp.sin(n * zi)
    ar, ai = pr[1], pi[1]
    den = lam_re * lam_re + lam_im * lam_im
    fr = ((ar - 1.0) * lam_re + ai * lam_im) / den
    fi = (ai * lam_re - (ar - 1.0) * lam_im) / den
    bbr = fr[..., None] * b_re - fi[..., None] * b_im
    bbi = fr[..., None] * b_im + fi[..., None] * b_re
    prt, pit = jnp.transpose(pr, (1, 2, 0, 3))[:, :, :, None, :], jnp.transpose(pi, (1, 2, 0, 3))[:, :, :, None, :]
    car = c_re[:, :, None] * prt - c_im[:, :, None] * pit
    cai = c_re[:, :, None] * pit + c_im[:, :, None] * prt
    kern = jnp.einsum('dgmq,dgqi->dgmi', jnp.concatenate([car, -cai], axis=-1).reshape(2, g, (t + 1) * h, 2 * p),
                      jnp.concatenate([bbr, bbi], axis=2), precision=HIGHEST).reshape(2, g, t + 1, h, h)
    ti = jnp.arange(t)
    lag = np.arange(t)[None, :] - np.arange(t)[:, None]
    place = np.stack([(lag == n) for n in range(t + 1)] + [(-lag == n) for n in range(t + 1)])
    place = place.reshape(2, t + 1, t, t).astype(np.float32)
    intra = jnp.einsum('dnst,dgnoi->gsito', place, kern, precision=HIGHEST).reshape(g, t * h, t * h)

    def inject(d, powers):
        er, ei = pr[powers, d], pi[powers, d]
        re = er[..., None] * bbr[d][None] - ei[..., None] * bbi[d][None]
        im = er[..., None] * bbi[d][None] + ei[..., None] * bbr[d][None]
        tr = lambda a: jnp.transpose(a, (1, 0, 3, 2)).reshape(g, t * h, p)
        return tr(re), tr(im)

    pad = lambda a: jnp.pad(a, ((0, 0), (0, 0), (0, LANES - p)))
    in_f = inject(0, t - 1 - ti)
    in_b = inject(1, ti)
    w1 = jnp.concatenate([intra] + [pad(a) for a in (*in_f, *in_b)], axis=-1)

    def readout(d, powers):
        tr = lambda a: jnp.transpose(a, (0, 3, 1, 2)).reshape(g, p, t * h)
        return tr(car[d][:, powers]), tr(-cai[d][:, powers])

    padr = lambda a: jnp.pad(a, ((0, 0), (0, LANES - p), (0, 0)))
    w2 = jnp.concatenate([padr(a) for a in (*readout(0, ti + 1), *readout(1, t - ti))], axis=1)
    dec = jnp.stack([pr[t, 0], pi[t, 0], pr[t, 1], pi[t, 1]], axis=1)
    dec = jnp.pad(dec, ((0, 0), (0, 4), (0, LANES - p)))
    dsk = jnp.tile(f(d_skip).reshape(g, 1, h), (1, t, 1)).reshape(g, 1, t * h)
    return w1.astype(BF16), w2.astype(BF16), dec, dsk


def _s5_kernel(hc_ref, hl_ref, pg_ref, pt_ref, w1_ref, w2_ref, dec_ref, dsk_ref, yc_ref, yl_ref,
               z_scr, h_scr, u_scr, *, bsz, ncc, ncl, width, row_blk):
    rc, rl = ncc * bsz, ncl * bsz
    t = hl_ref.shape[0]
    cb = max(1, row_blk // bsz)
    w1 = w1_ref[0]
    pg = pg_ref[0]

    @pl.when(pl.program_id(1) == 0)
    def _():
        yc_ref[...] = jnp.zeros(yc_ref.shape, BF16)
        yl_ref[...] = jnp.zeros(yl_ref.shape, BF16)

    def blocks():
        for h_ref, y_ref, base, nch in ((hc_ref, yc_ref, 0, ncc), (hl_ref, yl_ref, rc, ncl)):
            for c0 in range(0, nch, cb):
                n = min(cb, nch - c0)
                yield h_ref, y_ref, c0, n, slice(base + c0 * bsz, base + (c0 + n) * bsz)

    for h_ref, _, c0, n, sl in blocks():
        xcat = jnp.concatenate([h_ref[tt, c0:c0 + n].reshape(n * bsz, LANES) for tt in range(t)], axis=1)
        u = _dot(xcat, pg).astype(BF16)
        u_scr[sl, :] = u
        z_scr[sl, :] = _dot(u, w1)
    dec = dec_ref[0]
    zero = jnp.zeros((bsz, LANES), F32)
    cols = [width + k * LANES for k in range(5)]

    def make_step(are, aim, col, hcol):
        def step(row, carry):
            re, im = carry
            r = pl.multiple_of(row, bsz)
            h_scr[pl.ds(r, bsz), hcol:hcol + LANES] = re
            h_scr[pl.ds(r, bsz), hcol + LANES:hcol + 2 * LANES] = im
            s_re = z_scr[pl.ds(r, bsz), cols[col]:cols[col + 1]]
            s_im = z_scr[pl.ds(r, bsz), cols[col + 1]:cols[col + 2]]
            return are * re - aim * im + s_re, are * im + aim * re + s_im
        return step

    fstep = make_step(dec[0:1, :], dec[1:2, :], 0, 0)
    bstep = make_step(dec[2:3, :], dec[3:4, :], 2, 2 * LANES)
    lax.fori_loop(0, ncc + ncl, lambda c, s: fstep(c * bsz, s), (zero, zero))
    st = lax.fori_loop(0, ncc, lambda i, s: bstep((ncc - 1 - i) * bsz, s), (zero, zero))
    lax.fori_loop(0, ncl, lambda i, s: bstep((ncc + ncl - 1 - i) * bsz, s), st)

    w2 = w2_ref[0]
    dsk = dsk_ref[0]
    pt = pt_ref[0]
    for _, y_ref, c0, n, sl in blocks():
        y = z_scr[sl, 0:width] + _dot(h_scr[sl, :].astype(BF16), w2)
        y = jax.nn.gelu(y + dsk * u_scr[sl, :].astype(F32)).astype(BF16)
        yn = _dot(y, pt)
        for tt in range(t):
            slab = yn[:, tt * LANES:(tt + 1) * LANES].astype(BF16).reshape(n, bsz, LANES)
            y_ref[tt, c0:c0 + n] = y_ref[tt, c0:c0 + n] + slab


def _s5_lane_perm(t, hch):
    ng = LANES // hch
    pg = np.zeros((ng, t, LANES, t, hch), np.float32)
    for g in range(ng):
        for tt in range(t):
            pg[g, tt, g * hch + np.arange(hch), tt, np.arange(hch)] = 1.0
    return pg.reshape(ng, t * LANES, t * hch)


def _s5_core(hc, hl, tables, row_blk=512):
    w1, w2, dec, dsk = tables
    t, ncc, bsz, d = hc.shape
    ncl = hl.shape[1]
    width = w2.shape[2]
    hch = width // t
    ng = LANES // hch
    rows = (ncc + ncl) * bsz
    pg = _s5_lane_perm(t, hch)
    kern = functools.partial(_s5_kernel, bsz=bsz, ncc=ncc, ncl=ncl, width=width, row_blk=row_blk)
    gmap = lambda j, q: (j * ng + q, 0, 0)
    qmap = lambda j, q: (q, 0, 0)
    once = pl.Buffered(1)
    return pl.pallas_call(
        kern,
        grid=(d // LANES, ng),
        in_specs=[pl.BlockSpec((t, ncc, bsz, LANES), lambda j, q: (0, 0, 0, j), pipeline_mode=once),
                  pl.BlockSpec((t, ncl, bsz, LANES), lambda j, q: (0, 0, 0, j), pipeline_mode=once),
                  pl.BlockSpec((1,) + pg.shape[1:], qmap), pl.BlockSpec((1, t * hch, t * LANES), qmap),
                  pl.BlockSpec((1,) + w1.shape[1:], gmap), pl.BlockSpec((1,) + w2.shape[1:], gmap),
                  pl.BlockSpec((1,) + dec.shape[1:], gmap), pl.BlockSpec((1,) + dsk.shape[1:], gmap)],
        out_specs=[pl.BlockSpec((t, ncc, bsz, LANES), lambda j, q: (0, 0, 0, j), pipeline_mode=once),
                   pl.BlockSpec((t, ncl, bsz, LANES), lambda j, q: (0, 0, 0, j), pipeline_mode=once)],
        out_shape=[jax.ShapeDtypeStruct(hc.shape, BF16), jax.ShapeDtypeStruct(hl.shape, BF16)],
        scratch_shapes=[pltpu.VMEM((rows, w1.shape[2]), F32), pltpu.VMEM((rows, 4 * LANES), F32),
                        pltpu.VMEM((rows, width), BF16)],
        compiler_params=_params("parallel", "arbitrary"),
        name="s5_core",
    )(hc, hl, jnp.asarray(pg, dtype=BF16), jnp.asarray(pg.transpose(0, 2, 1), dtype=BF16), w1, w2, dec, dsk)


def _time_major(hm, bsz, t):
    n, d = hm.shape
    return jnp.transpose(hm.reshape(bsz, n // bsz // t, t, d), (2, 1, 0, 3))


def _token_major(y):
    t, nc, bsz, d = y.shape
    return jnp.transpose(y, (2, 1, 0, 3)).reshape(bsz * nc * t, d)


def _mixout_kernel(y_ref, w_ref, x_ref, mod_ref, o_ref, *, glu):
    r = _dot(y_ref[...], w_ref[...])
    if glu:
        half = r.shape[-1] // 2
        r = r[:, :half] * jax.nn.sigmoid(r[:, half:])
    o_ref[...] = x_ref[...] + mod_ref[0, 2:3, :] * r


def _mixout(y, w, x, mod, tile, tiles_per_mod, glu):
    n, d = x.shape
    k, nn = w.shape
    return pl.pallas_call(
        functools.partial(_mixout_kernel, glu=glu),
        grid=(n // tile,),
        in_specs=[pl.BlockSpec((tile, k), lambda i: (i, 0)),
                  pl.BlockSpec((k, nn), lambda i: (0, 0)),
                  pl.BlockSpec((tile, d), lambda i: (i, 0)),
                  pl.BlockSpec((1, 6, d), lambda i: (i // tiles_per_mod, 0, 0))],
        out_specs=pl.BlockSpec((tile, d), lambda i: (i, 0)),
        out_shape=jax.ShapeDtypeStruct((n, d), F32),
        compiler_params=_params("parallel"),
        name="mixer_out",
    )(y, w, x, mod)


def _mlp_kernel(x_ref, nw_ref, mod_ref, w1_ref, w2_ref, fw_ref, o_ref, *, ff_blk, final):
    x = x_ref[...]
    h = _norm_mod(x, nw_ref[...], mod_ref[0, 3:4, :], mod_ref[0, 4:5, :]).astype(BF16)
    acc = jnp.zeros(x.shape, F32)
    for f0 in range(0, w1_ref.shape[1], ff_blk):
        a = jnp.maximum(_dot(h, w1_ref[:, f0:f0 + ff_blk]), 0.0)
        acc = acc + _dot((a * a).astype(BF16), w2_ref[f0:f0 + ff_blk, :])
    y = x + mod_ref[0, 5:6, :] * acc
    if final:
        y = y * lax.rsqrt(jnp.mean(y * y, axis=-1, keepdims=True) + NORM_EPS) * fw_ref[...]
    o_ref[...] = y


def _mlp(x, nw, mod, w1, w2, fw, tile, tiles_per_mod, final):
    n, d = x.shape
    ff = w1.shape[1]
    return pl.pallas_call(
        functools.partial(_mlp_kernel, ff_blk=min(1024, ff), final=final),
        grid=(n // tile,),
        in_specs=[pl.BlockSpec((tile, d), lambda i: (i, 0)),
                  pl.BlockSpec((1, d), lambda i: (0, 0)),
                  pl.BlockSpec((1, 6, d), lambda i: (i // tiles_per_mod, 0, 0)),
                  pl.BlockSpec((d, ff), lambda i: (0, 0)),
                  pl.BlockSpec((ff, d), lambda i: (0, 0)),
                  pl.BlockSpec((1, d), lambda i: (0, 0))],
        out_specs=pl.BlockSpec((tile, d), lambda i: (i, 0)),
        out_shape=jax.ShapeDtypeStruct((n, d), F32),
        compiler_params=_params("parallel"),
        name="mlp",
    )(x, nw.reshape(1, d), mod, w1, w2, fw.reshape(1, d))


def _outmlp_kernel(y_ref, wo_ref, x_ref, nw_ref, mod_ref, w1_ref, w2_ref, fw_ref, o_ref, *, ff_blk):
    x = x_ref[...] + mod_ref[0, 2:3, :] * _dot(y_ref[...], wo_ref[...])
    h = _norm_mod(x, nw_ref[...], mod_ref[0, 3:4, :], mod_ref[0, 4:5, :]).astype(BF16)
    acc = jnp.zeros(x.shape, F32)
    for f0 in range(0, w1_ref.shape[1], ff_blk):
        a = jnp.maximum(_dot(h, w1_ref[:, f0:f0 + ff_blk]), 0.0)
        acc = acc + _dot((a * a).astype(BF16), w2_ref[f0:f0 + ff_blk, :])
    y = x + mod_ref[0, 5:6, :] * acc
    o_ref[...] = y * lax.rsqrt(jnp.mean(y * y, axis=-1, keepdims=True) + NORM_EPS) * fw_ref[...]


def _outmlp(y, wo, x, nw, mod, w1, w2, fw, tile, tiles_per_mod):
    n, d = x.shape
    k = y.shape[1]
    ff = w1.shape[1]
    once = pl.Buffered(1)
    return pl.pallas_call(
        functools.partial(_outmlp_kernel, ff_blk=min(1024, ff)),
        grid=(n // tile,),
        in_specs=[pl.BlockSpec((tile, k), lambda i: (i, 0)),
                  pl.BlockSpec((k, d), lambda i: (0, 0), pipeline_mode=once),
                  pl.BlockSpec((tile, d), lambda i: (i, 0)),
                  pl.BlockSpec((1, d), lambda i: (0, 0)),
                  pl.BlockSpec((1, 6, d), lambda i: (i // tiles_per_mod, 0, 0)),
                  pl.BlockSpec((d, ff), lambda i: (0, 0), pipeline_mode=once),
                  pl.BlockSpec((ff, d), lambda i: (0, 0), pipeline_mode=once),
                  pl.BlockSpec((1, d), lambda i: (0, 0))],
        out_specs=pl.BlockSpec((tile, d), lambda i: (i, 0)),
        out_shape=jax.ShapeDtypeStruct((n, d), F32),
        compiler_params=_params("parallel"),
        name="out_mlp",
    )(y, wo, x, nw.reshape(1, d), mod, w1, w2, fw.reshape(1, d))


def _gdn_in_kernel(x_ref, nw_ref, mod_ref, w_ref, wg_ref, cw_ref, gp_ref, p_ref, g_ref, gt_ref, acc_scr,
                   *, period, col0, qk, vd, tn, hv, hdim):
    tb = x_ref.shape[0]
    h = _norm_mod(x_ref[...], nw_ref[...], mod_ref[0, 0:1, :], mod_ref[0, 1:2, :])
    hb = h.astype(BF16)
    gl = _dot(h, wg_ref[...], HIGHEST)
    lane = lax.broadcasted_iota(jnp.int32, (1, LANES), 1)
    beta = jax.nn.sigmoid(gl)
    t = gl + gp_ref[1:2, :]
    g = gp_ref[0:1, :] * (jnp.maximum(t, 0.0) + jnp.log1p(jnp.exp(-jnp.abs(t))))
    r = lax.broadcasted_iota(jnp.int32, (tb, tb), 0)
    c = lax.broadcasted_iota(jnp.int32, (tb, tb), 1)
    same = _floor_pow2(r, GDN_CHUNK) == _floor_pow2(c, GDN_CHUNK)
    gcf = _dot(jnp.where(same & (c <= r), 1.0, 0.0), g, HIGHEST)
    gcb = _dot(jnp.where(same & (c >= r), 1.0, 0.0), g, HIGHEST)
    out = jnp.where(lane < 2 * hv, beta, jnp.where(lane < 3 * hv, gcf, jnp.where(lane < 4 * hv, gcb, 0.0)))
    g_ref[...] = out
    for cc in range(tb // GDN_CHUNK):
        gt_ref[cc] = out[cc * GDN_CHUNK:(cc + 1) * GDN_CHUNK, :].T[2 * hv:4 * hv, :]

    assert tb % period == 0
    row = lax.broadcasted_iota(jnp.int32, (period, 1), 0)
    edge = [(tap, ((row + tap - 2 >= 0) & (row + tap - 2 < period)).astype(F32)) for tap in (0, 1, 3, 4)]

    for bi, o0 in enumerate(range(0, p_ref.shape[1], tn)):
        c0 = col0 + o0
        acc = _dot(hb, w_ref[:, c0:c0 + tn])
        if qk <= c0 < qk + vd:
            p_ref[:, o0:o0 + tn] = acc.astype(BF16)
            continue
        acc_ref = acc_scr.at[bi % 2]
        acc_ref[...] = acc
        scale = hdim ** -0.5 if c0 < qk else (1.0 if c0 < 2 * qk + vd else None)
        for h0 in range(0, tn, hdim):
            cw = cw_ref[:, c0 + h0:c0 + h0 + hdim]
            taps = [(tap, ok * cw[tap:tap + 1, :]) for tap, ok in edge]
            for r0 in range(0, tb, period):
                a = acc_ref[r0:r0 + period, h0:h0 + hdim]
                y = a * cw[2:3, :]
                for tap, wt in taps:
                    y = y + pltpu.roll(a, (period - (tap - 2)) % period, 0) * wt
                y = _silu(y)
                if scale is not None:
                    y = y * (lax.rsqrt(jnp.sum(y * y, axis=-1, keepdims=True) + NORM_EPS) * scale)
                p_ref[r0:r0 + period, o0 + h0:o0 + h0 + hdim] = y.astype(BF16)


def _gdn_in(x, nw, mod, w_main, w_gate, cw, gp, tile, tiles_per_mod, period, col0, tn, qk, vd, hv, hdim):
    n, d = x.shape
    n_main = w_main.shape[1]
    kern = functools.partial(_gdn_in_kernel, period=period, col0=col0, qk=qk, vd=vd, tn=tn, hv=hv, hdim=hdim)
    return pl.pallas_call(
        kern,
        grid=(n // tile,),
        in_specs=[pl.BlockSpec((tile, d), lambda i: (i, 0)),
                  pl.BlockSpec((1, d), lambda i: (0, 0)),
                  pl.BlockSpec((1, 6, d), lambda i: (i // tiles_per_mod, 0, 0)),
                  pl.BlockSpec((d, n_main), lambda i: (0, 0)),
                  pl.BlockSpec((d, LANES), lambda i: (0, 0)),
                  pl.BlockSpec((8, n_main), lambda i: (0, 0)),
                  pl.BlockSpec((2, LANES), lambda i: (0, 0))],
        out_specs=[pl.BlockSpec((tile, n_main - col0), lambda i: (i, 0)),
                   pl.BlockSpec((tile, LANES), lambda i: (i, 0)),
                   pl.BlockSpec((tile // GDN_CHUNK, 2 * hv, GDN_CHUNK), lambda i: (i, 0, 0))],
        out_shape=[jax.ShapeDtypeStruct((n, n_main - col0), BF16),
                   jax.ShapeDtypeStruct((n, LANES), F32),
                   jax.ShapeDtypeStruct((n // GDN_CHUNK, 2 * hv, GDN_CHUNK), F32)],
        scratch_shapes=[pltpu.VMEM((2, tile, tn), F32)],
        compiler_params=_params("parallel"),
        name="gdn_in",
    )(x, nw.reshape(1, d), mod, w_main, w_gate, cw, gp)


def _gdn_masks(ch, rep):
    nco = 2 * rep
    r, c = np.meshgrid(np.arange(ch), np.arange(ch), indexing="ij")
    per_dir = []
    for lower in (True, False):
        m = [(c <= r) if lower else (c >= r), (c < r) if lower else (c > r)]
        b = 1
        while b < ch:
            same = (r // (2 * b)) == (c // (2 * b))
            hi_r, hi_c = (r % (2 * b)) >= b, (c % (2 * b)) >= b
            m.append(same & ((hi_r & ~hi_c) if lower else (~hi_r & hi_c)))
            b *= 2
        per_dir.append(np.stack(m))
    cm = np.concatenate([per_dir[j // rep] for j in range(nco)], axis=2).astype(np.float32)
    blk = np.arange(nco * ch) // ch
    return cm, (blk[:, None] == blk[None, :]).astype(np.float32)


def _gdn_kernel(q_ref, k_ref, v_ref, z_ref, kc_ref, vc_ref, g_ref, gt_ref, gc_ref, gct_ref, ow_ref, cm_ref, lv_ref, bd_ref,
                o_ref, of_s, ob_s, st_s, uw_s, in_s, kt_s, *, hv, khb, rep, hdim, ncc, ncl, unroll):
    ch = GDN_CHUNK
    nco = 2 * rep
    nlvl = lv_ref.shape[0]
    kh0 = pl.program_id(1) * khb
    lane = lax.broadcasted_iota(jnp.int32, (1, LANES), 1)
    lcat = lax.broadcasted_iota(jnp.int32, (1, nco * ch), 1)

    def lanes_of(khl, j):
        head = (kh0 + khl) * rep + j % rep
        return (j // rep) * hv + head, (2 + j // rep) * hv + head

    def column(gt, idx):
        return jnp.sum(jnp.where(lane == idx, gt, 0.0), axis=1, keepdims=True)

    def spread(cols):
        out = jnp.broadcast_to(cols[nco - 1], (ch, nco * ch))
        for j in range(nco - 2, -1, -1):
            out = jnp.where(lcat < (j + 1) * ch, cols[j], out)
        return out

    def block_diag(xb):
        return jnp.concatenate([xb] * nco, axis=0) * bd_ref[...]

    def prepare(insts):
        st = []
        for khl, k_r, q_r, v_r, g_r, gt_r, c, slot in insts:
            rows = pl.ds(pl.multiple_of(c * ch, ch), ch)
            kb = k_r[rows, khl * hdim:(khl + 1) * hdim]
            kcat = jnp.concatenate([kb] * nco, axis=0)
            if q_r is None:
                kkc, qkc = _dot_nt(kb, kcat), None
            else:
                prod = _dot_nt(jnp.concatenate([kb, q_r[rows, khl * hdim:(khl + 1) * hdim]], axis=0), kcat)
                kkc, qkc = prod[:ch], prod[ch:]
            gtile = g_r[rows, :]
            betas, gcols, grows = [], [], []
            for j in range(nco):
                bl, gl = lanes_of(khl, j)
                betas.append(column(gtile, bl))
                gcols.append(column(gtile, gl))
                grows.append(gt_r[c, pl.ds(gl - 2 * hv, 1), :])
            dlog = spread(gcols) - jnp.concatenate(grows, axis=1)
            decay = jnp.exp(jnp.where(cm_ref[0] > 0.0, dlog, -jnp.inf))
            ac = spread(betas) * kkc * decay * cm_ref[1]
            xb = ((cm_ref[0] - cm_ref[1]) - ac * cm_ref[2]).astype(BF16)
            st.append([rows, kb, qkc, betas, gcols, decay, ac.astype(BF16), xb])
        for lvl in range(1, nlvl):
            ps = [_dot(s[6] * lv_ref[lvl], block_diag(s[7])) for s in st]
            for s, p in zip(st, ps):
                s[7] = s[7] - _dot(s[7], block_diag(p.astype(BF16))).astype(BF16)
        outs = []
        for (khl, k_r, q_r, v_r, g_r, gt_r, c, slot), (rows, kb, qkc, betas, gcols, decay, ac, xc) in zip(insts, st):
            kf = kb.astype(F32)
            rst = []
            for j in range(nco):
                v0 = (khl * rep + j % rep) * hdim
                vf = v_r[rows, v0:v0 + hdim].astype(F32)
                rst.append(jnp.concatenate([vf * betas[j], kf * (betas[j] * jnp.exp(gcols[j]))], axis=1))
            uw = _dot(block_diag(xc), jnp.concatenate(rst, axis=0).astype(BF16))
            intra = None if qkc is None else (qkc * decay * cm_ref[0]).astype(BF16)
            outs.append((uw, kf.T.astype(BF16), intra))
        for (khl, k_r, q_r, v_r, g_r, gt_r, c, slot), (uw, kt, intra) in zip(insts, outs):
            for j in range(nco):
                uw_s[slot, khl * nco + j] = uw[j * ch:(j + 1) * ch, :].astype(BF16)
                if intra is not None:
                    in_s[c, khl * nco + j] = intra[:, j * ch:(j + 1) * ch]
            kt_s[slot, khl] = kt

    def advance(items):
        mid = []
        for khl, j, q_r, g_r, gt_r, c, slot in items:
            rows = pl.ds(pl.multiple_of(c * ch, ch), ch)
            _, gl = lanes_of(khl, j)
            gcol = column(g_r[rows, :], gl)
            grow = gt_r[c, pl.ds(gl - 2 * hv, 1), :]
            glast = grow[:, ch - 1:ch] if j < rep else grow[:, 0:1]
            sidx = khl * nco + j
            s = st_s[sidx]
            lhs = uw_s[slot, sidx, :, hdim:]
            if q_r is not None:
                qg = q_r[rows, khl * hdim:(khl + 1) * hdim].astype(F32) * jnp.exp(gcol)
                lhs = jnp.concatenate([lhs, qg.astype(BF16)], axis=0)
            mid.append((rows, gcol, glast, sidx, s, _dot(lhs, s.astype(BF16))))
        res = []
        for (khl, j, q_r, g_r, gt_r, c, slot), (rows, gcol, glast, sidx, s, ws) in zip(items, mid):
            vnew = uw_s[slot, sidx, :, :hdim].astype(F32) - ws[:ch]
            snew = s * jnp.exp(glast) + _dot(kt_s[slot, khl], (vnew * jnp.exp(glast - gcol)).astype(BF16))
            o = None if q_r is None else ws[ch:] + _dot(in_s[c, sidx], vnew.astype(BF16))
            res.append((snew, o))
        for (khl, j, q_r, g_r, gt_r, c, slot), (rows, gcol, glast, sidx, s, ws), (snew, o) in zip(items, mid, res):
            st_s[sidx] = snew
            if o is not None:
                v0 = (khl * rep + j % rep) * hdim
                (of_s if j < rep else ob_s)[rows, v0:v0 + hdim] = o.astype(BF16)

    st_s[...] = jnp.zeros(st_s.shape, F32)
    uc = min(unroll, ncc)

    def prep_ctx(i, carry):
        prepare([(khl, kc_ref, None, vc_ref, gc_ref, gct_ref, i * uc + t, i * uc + t)
                 for t in range(uc) for khl in range(khb)])
        return carry

    def prep_lat(i, carry):
        prepare([(khl, k_ref, q_ref, v_ref, g_ref, gt_ref, i * unroll + t, ncc + i * unroll + t)
                 for t in range(unroll) for khl in range(khb)])
        return carry

    def adv_ctx(i, carry):
        advance([(khl, j, None, gc_ref, gct_ref, i if j < rep else ncc - 1 - i, i if j < rep else ncc - 1 - i)
                 for khl in range(khb) for j in range(nco)])
        return carry

    def adv_lat(i, carry):
        advance([(khl, j, q_ref, g_ref, gt_ref, i if j < rep else ncl - 1 - i,
                  ncc + (i if j < rep else ncl - 1 - i)) for khl in range(khb) for j in range(nco)])
        return carry

    lax.fori_loop(0, ncc // uc, prep_ctx, 0)
    lax.fori_loop(0, ncl // unroll, prep_lat, 0)
    lax.fori_loop(0, ncc, adv_ctx, 0)
    lax.fori_loop(0, ncl, adv_lat, 0)

    blk = min(256, ncl * ch)
    ow = ow_ref[...]
    for r0 in range(0, ncl * ch, blk):
        for v0 in range(0, khb * rep * hdim, hdim):
            o = of_s[r0:r0 + blk, v0:v0 + hdim].astype(F32) + ob_s[r0:r0 + blk, v0:v0 + hdim].astype(F32)
            o = o * lax.rsqrt(jnp.mean(o * o, axis=-1, keepdims=True) + NORM_EPS) * ow
            o_ref[r0:r0 + blk, v0:v0 + hdim] = (o * _silu(z_ref[r0:r0 + blk, v0:v0 + hdim].astype(F32))).astype(BF16)


def _gdn_core(p_lat, p_ctx, g_lat, gt_lat, g_ctx, gt_ctx, onorm_w, bsz, hk, hv, hdim, khb=GDN_KEY_HEADS_PER_STEP):
    l = p_lat.shape[0] // bsz
    lc = p_ctx.shape[0] // bsz
    ch = GDN_CHUNK
    ncc, ncl = lc // ch, l // ch
    rep = hv // hk
    nco = 2 * rep
    kw, vw = khb * hdim, khb * rep * hdim
    qk, vd = hk * hdim, hv * hdim
    unroll = min(GDN_PREP_UNROLL, ncl)
    assert hk % khb == 0 and ncl % unroll == 0 and ncc % min(unroll, ncc) == 0
    assert qk % vw == 0 and vd % vw == 0
    kern = functools.partial(_gdn_kernel, hv=hv, khb=khb, rep=rep, hdim=hdim, ncc=ncc, ncl=ncl, unroll=unroll)
    kcol, vcol, zcol = (qk + vd) // kw, (2 * qk + vd) // vw, qk // vw
    cm, bd = _gdn_masks(ch, rep)
    return pl.pallas_call(
        kern,
        grid=(bsz, hk // khb),
        in_specs=[pl.BlockSpec((l, kw), lambda b, h: (b, h)),
                  pl.BlockSpec((l, kw), lambda b, h: (b, kcol + h)),
                  pl.BlockSpec((l, vw), lambda b, h: (b, vcol + h)),
                  pl.BlockSpec((l, vw), lambda b, h: (b, zcol + h)),
                  pl.BlockSpec((lc, kw), lambda b, h: (b, h)),
                  pl.BlockSpec((lc, vw), lambda b, h: (b, qk // vw + h)),
                  pl.BlockSpec((l, LANES), lambda b, h: (b, 0)),
                  pl.BlockSpec((ncl, 2 * hv, ch), lambda b, h: (b, 0, 0)),
                  pl.BlockSpec((lc, LANES), lambda b, h: (b, 0)),
                  pl.BlockSpec((ncc, 2 * hv, ch), lambda b, h: (b, 0, 0)),
                  pl.BlockSpec((1, hdim), lambda b, h: (0, 0)),
                  pl.BlockSpec(cm[:3].shape, lambda b, h: (0, 0, 0)),
                  pl.BlockSpec(cm[2:].shape, lambda b, h: (0, 0, 0)),
                  pl.BlockSpec(bd.shape, lambda b, h: (0, 0))],
        out_specs=pl.BlockSpec((l, vw), lambda b, h: (b, h)),
        out_shape=jax.ShapeDtypeStruct((bsz * l, vd), BF16),
        scratch_shapes=[pltpu.VMEM((l, vw), BF16), pltpu.VMEM((l, vw), BF16),
                        pltpu.VMEM((khb * nco, hdim, hdim), F32),
                        pltpu.VMEM((ncc + ncl, khb * nco, ch, 2 * hdim), BF16),
                        pltpu.VMEM((ncl, khb * nco, ch, ch), BF16),
                        pltpu.VMEM((ncc + ncl, khb, hdim, ch), BF16)],
        compiler_params=_params("parallel", "arbitrary"),
        name="gdn_core",
    )(p_lat, p_lat, p_lat, p_lat, p_ctx, p_ctx, g_lat, gt_lat, g_ctx, gt_ctx, onorm_w.reshape(1, hdim),
      jnp.asarray(cm[:3]), jnp.asarray(cm[2:], dtype=BF16), jnp.asarray(bd, dtype=BF16))


def kernel(x, c, ctx, c_ctx, ada_w, ada_b, norm1_w, norm2_w, mlp_w1, mlp_w2, s5_lam_re, s5_lam_im, s5_log_dt, s5_b_re, s5_b_im, s5_c_re, s5_c_im, s5_d, s5_w_glu, gdn_w_in, gdn_conv_w, gdn_a_log, gdn_dt_bias, gdn_onorm_w, gdn_w_out, final_norm_w):
    bsz, l, d = x.shape
    lc = ctx.shape[1]
    depth = ada_w.shape[0]
    assert depth == 2 and s5_lam_re.shape[0] == 1 and gdn_w_in.shape[0] == 1
    hv = gdn_a_log.shape[2]
    hdim = gdn_onorm_w.shape[1]
    vd = gdn_w_out.shape[1]
    qk = (gdn_conv_w.shape[2] - vd) // 2
    hk = qk // hdim
    assert l % GRID_W == 0 and l % GDN_CHUNK == 0 and lc % GDN_CHUNK == 0
    assert l % S5_CHUNK == 0 and lc % S5_CHUNK == 0 and 4 * hv <= LANES

    tb_l, tb_c = min(256, l), lc
    tm_l, tm_c = min(512, l), min(512, bsz * lc)
    tx_l = min(1024, l)
    xl = x.reshape(bsz * l, d)
    xc = ctx.reshape(bsz * lc, d)

    rows = -(-(bsz + 1) // 8) * 8
    cvec = jnp.zeros((rows, d), F32).at[:bsz].set(c).at[bsz].set(c_ctx)
    mod = _ada_mod(cvec, ada_w, ada_b).reshape(depth, rows, 6, d)
    big = 1 << 30

    mod_l, mod_c = mod[0, :bsz], mod[0, bsz:bsz + 1]
    hl = _prenorm(xl, norm1_w[0], mod_l, tx_l, l // tx_l)
    hc = _prenorm(xc, norm1_w[0], mod_c, tb_c, big)
    tables = _s5_tables(s5_lam_re[0], s5_lam_im[0], s5_log_dt[0], s5_b_re[0], s5_b_im[0],
                        s5_c_re[0], s5_c_im[0], s5_d[0], S5_CHUNK)
    yc, yl = _s5_core(_time_major(hc, bsz, S5_CHUNK), _time_major(hl, bsz, S5_CHUNK), tables)
    w_glu = s5_w_glu[0].astype(BF16)
    w1, w2 = mlp_w1[0].astype(BF16), mlp_w2[0].astype(BF16)
    xl = _mixout(_token_major(yl), w_glu, xl, mod_l, tx_l, l // tx_l, True)
    xc = _mixout(_token_major(yc), w_glu, xc, mod_c, tm_c, big, True)
    xl = _mlp(xl, norm2_w[0], mod_l, w1, w2, final_norm_w, tm_l, l // tm_l, False)
    xc = _mlp(xc, norm2_w[0], mod_c, w1, w2, final_norm_w, tm_c, big, False)

    mod_l, mod_c = mod[1, :bsz], mod[1, bsz:bsz + 1]
    w_in = gdn_w_in[0]
    n_main = 2 * qk + 2 * vd
    w_main = w_in[:, :n_main].astype(BF16)
    w_gate = jnp.pad(w_in[:, n_main:], ((0, 0), (0, LANES - 4 * hv)))
    conv_w = gdn_conv_w[0].astype(F32)
    cw = jnp.concatenate([conv_w[:, :qk], jnp.zeros((conv_w.shape[0], vd), F32), conv_w[:, qk:]], axis=1)
    cw = jnp.pad(cw, ((0, 8 - cw.shape[0]), (0, 0)))
    neg_a = jnp.pad(-jnp.exp(gdn_a_log[0].astype(F32)).reshape(-1), (2 * hv, LANES - 4 * hv))
    dtb = jnp.pad(gdn_dt_bias[0].astype(F32).reshape(-1), (2 * hv, LANES - 4 * hv))
    gp = jnp.stack([neg_a, dtb])
    tn = min(512, qk)
    p_l, g_l, gt_l = _gdn_in(xl, norm1_w[1], mod_l, w_main, w_gate, cw, gp, tb_l, l // tb_l, GRID_W,
                             0, tn, qk, vd, hv, hdim)
    p_c, g_c, gt_c = _gdn_in(xc, norm1_w[1], mod_c, w_main, w_gate, cw, gp, tb_c, big, lc,
                             qk + vd, tn, qk, vd, hv, hdim)
    gated = _gdn_core(p_l, p_c, g_l, gt_l, g_c, gt_c, gdn_onorm_w[0].astype(F32), bsz, hk, hv, hdim)
    out = _outmlp(gated, gdn_w_out[0].astype(BF16), xl, norm2_w[1], mod_l, mlp_w1[1].astype(BF16),
                  mlp_w2[1].astype(BF16), final_norm_w, tm_l, l // tm_l)
    return out.reshape(bsz, l, d)
```

```python
import functools

import jax
import jax.numpy as jnp
import numpy as np
from jax import lax
from jax.experimental import pallas as pl
from jax.experimental.pallas import tpu as pltpu

F32 = jnp.float32
BF16 = jnp.bfloat16
HIGHEST = lax.Precision.HIGHEST

NORM_EPS = 1e-6
GRID_W = 64
GDN_CHUNK = 64
GDN_KEY_HEADS_PER_STEP = 2
GDN_PREP_UNROLL = 8
S5_CHUNK = 16
LANES = 128
VMEM_LIMIT = 56 * 1024 * 1024


def _dot(a, b, precision=None):
    return jnp.dot(a, b, preferred_element_type=F32, precision=precision)


def _dot_nt(a, b):
    return lax.dot_general(a, b, (((1,), (1,)), ((), ())), preferred_element_type=F32)


def _dot_tn(a, b):
    return lax.dot_general(a, b, (((0,), (0,)), ((), ())), preferred_element_type=F32)


def _floor_pow2(v, n):
    assert n & (n - 1) == 0
    return jnp.bitwise_and(v, -n)


def _silu(t):
    return t * jax.nn.sigmoid(t)


def _norm_mod(x, nw, shift, scale):
    y = x * lax.rsqrt(jnp.mean(x * x, axis=-1, keepdims=True) + NORM_EPS) * nw
    return y * (1.0 + scale) + shift


def _params(*sem):
    return pltpu.CompilerParams(dimension_semantics=sem, vmem_limit_bytes=VMEM_LIMIT)


def _ada_kernel(c_ref, w_ref, b_ref, o_ref):
    o_ref[0] = _dot(_silu(c_ref[...]), w_ref[0], HIGHEST) + b_ref[0]


def _ada_mod(cvec, ada_w, ada_b):
    depth, d, n = ada_w.shape
    rows = cvec.shape[0]
    tn = n // 4
    return pl.pallas_call(
        _ada_kernel,
        grid=(depth, n // tn),
        in_specs=[pl.BlockSpec((rows, d), lambda i, j: (0, 0)),
                  pl.BlockSpec((1, d, tn), lambda i, j: (i, 0, j)),
                  pl.BlockSpec((1, 1, tn), lambda i, j: (i, 0, j))],
        out_specs=pl.BlockSpec((1, rows, tn), lambda i, j: (i, 0, j)),
        out_shape=jax.ShapeDtypeStruct((depth, rows, n), F32),
        compiler_params=_params("arbitrary", "arbitrary"),
        name="ada_mod",
    )(cvec, ada_w, ada_b.reshape(depth, 1, n))


def _prenorm_kernel(x_ref, nw_ref, mod_ref, o_ref):
    h = _norm_mod(x_ref[...], nw_ref[...], mod_ref[0, 0:1, :], mod_ref[0, 1:2, :])
    o_ref[...] = h.astype(o_ref.dtype)


def _prenorm(x, nw, mod, tile, tiles_per_mod):
    n, d = x.shape
    return pl.pallas_call(
        _prenorm_kernel,
        grid=(n // tile,),
        in_specs=[pl.BlockSpec((tile, d), lambda i: (i, 0)),
                  pl.BlockSpec((1, d), lambda i: (0, 0)),
                  pl.BlockSpec((1, 6, d), lambda i: (i // tiles_per_mod, 0, 0))],
        out_specs=pl.BlockSpec((tile, d), lambda i: (i, 0)),
        out_shape=jax.ShapeDtypeStruct((n, d), BF16),
        compiler_params=_params("parallel"),
        name="prenorm",
    )(x, nw.reshape(1, d), mod)


def _s5_tables(lam_re, lam_im, log_dt, b_re, b_im, c_re, c_im, d_skip, t):
    _, g, p = lam_re.shape
    h = b_re.shape[-1]
    f = lambda a: a.astype(F32)
    lam_re, lam_im, b_re, b_im, c_re, c_im = map(f, (lam_re, lam_im, b_re, b_im, c_re, c_im))
    dt = jnp.exp(f(log_dt))[..., None]
    zr, zi = lam_re * dt, lam_im * dt
    n = jnp.arange(t + 1, dtype=F32)[:, None, None, None]
    mag = jnp.exp(n * zr)
    pr, pi = mag * jnp.cos(n * zi), mag * jnp.sin(n * zi)
    ar, ai = pr[1], pi[1]
    den = lam_re * lam_re + lam_im * lam_im
    fr = ((ar - 1.0) * lam_re + ai * lam_im) / den
    fi = (ai * lam_re - (ar - 1.0) * lam_im) / den
    bbr = fr[..., None] * b_re - fi[..., None] * b_im
    bbi = fr[..., None] * b_im + fi[..., None] * b_re
    prt, pit = jnp.transpose(pr, (1, 2, 0, 3))[:, :, :, None, :], jnp.transpose(pi, (1, 2, 0, 3))[:, :, :, None, :]
    car = c_re[:, :, None] * prt - c_im[:, :, None] * pit
    cai = c_re[:, :, None] * pit + c_im[:, :, None] * prt
    kern = jnp.einsum('dgmq,dgqi->dgmi', jnp.concatenate([car, -cai], axis=-1).reshape(2, g, (t + 1) * h, 2 * p),
                      jnp.concatenate([bbr, bbi], axis=2), precision=HIGHEST).reshape(2, g, t + 1, h, h)
    ti = jnp.arange(t)
    lag = np.arange(t)[None, :] - np.arange(t)[:, None]
    place = np.stack([(lag == n) for n in range(t + 1)] + [(-lag == n) for n in range(t + 1)])
    place = place.reshape(2, t + 1, t, t).astype(np.float32)
    intra = jnp.einsum('dnst,dgnoi->gsito', place, kern, precision=HIGHEST).reshape(g, t * h, t * h)

    def inject(d, powers):
        er, ei = pr[powers, d], pi[powers, d]
        re = er[..., None] * bbr[d][None] - ei[..., None] * bbi[d][None]
        im = er[..., None] * bbi[d][None] + ei[..., None] * bbr[d][None]
        tr = lambda a: jnp.transpose(a, (1, 0, 3, 2)).reshape(g, t * h, p)
        return tr(re), tr(im)

    pad = lambda a: jnp.pad(a, ((0, 0), (0, 0), (0, LANES - p)))
    in_f = inject(0, t - 1 - ti)
    in_b = inject(1, ti)
    w1 = jnp.concatenate([intra] + [pad(a) for a in (*in_f, *in_b)], axis=-1)

    def readout(d, powers):
        tr = lambda a: jnp.transpose(a, (0, 3, 1, 2)).reshape(g, p, t * h)
        return tr(car[d][:, powers]), tr(-cai[d][:, powers])

    padr = lambda a: jnp.pad(a, ((0, 0), (0, LANES - p), (0, 0)))
    w2 = jnp.concatenate([padr(a) for a in (*readout(0, ti + 1), *readout(1, t - ti))], axis=1)
    dec = jnp.stack([pr[t, 0], pi[t, 0], pr[t, 1], pi[t, 1]], axis=1)
    dec = jnp.pad(dec, ((0, 0), (0, 4), (0, LANES - p)))
    dsk = jnp.tile(f(d_skip).reshape(g, 1, h), (1, t, 1)).reshape(g, 1, t * h)
    return w1.astype(BF16), w2.astype(BF16), dec, dsk


def _s5_kernel(hc_ref, hl_ref, pg_ref, pt_ref, w1_ref, w2_ref, dec_ref, dsk_ref, yc_ref, yl_ref,
               z_scr, h_scr, u_scr, *, bsz, ncc, ncl, width, row_blk):
    rc, rl = ncc * bsz, ncl * bsz
    t = hl_ref.shape[0]
    cb = max(1, row_blk // bsz)
    w1 = w1_ref[0]
    pg = pg_ref[0]

    @pl.when(pl.program_id(1) == 0)
    def _():
        yc_ref[...] = jnp.zeros(yc_ref.shape, BF16)
        yl_ref[...] = jnp.zeros(yl_ref.shape, BF16)

    def blocks():
        for h_ref, y_ref, base, nch in ((hc_ref, yc_ref, 0, ncc), (hl_ref, yl_ref, rc, ncl)):
            for c0 in range(0, nch, cb):
                n = min(cb, nch - c0)
                yield h_ref, y_ref, c0, n, slice(base + c0 * bsz, base + (c0 + n) * bsz)

    for h_ref, _, c0, n, sl in blocks():
        xcat = jnp.concatenate([h_ref[tt, c0:c0 + n].reshape(n * bsz, LANES) for tt in range(t)], axis=1)
        u = _dot(xcat, pg).astype(BF16)
        u_scr[sl, :] = u
        z_scr[sl, :] = _dot(u, w1)
    dec = dec_ref[0]
    zero = jnp.zeros((bsz, LANES), F32)
    cols = [width + k * LANES for k in range(5)]

    def make_step(are, aim, col, hcol):
        def step(row, carry):
            re, im = carry
            r = pl.multiple_of(row, bsz)
            h_scr[pl.ds(r, bsz), hcol:hcol + LANES] = re
            h_scr[pl.ds(r, bsz), hcol + LANES:hcol + 2 * LANES] = im
            s_re = z_scr[pl.ds(r, bsz), cols[col]:cols[col + 1]]
            s_im = z_scr[pl.ds(r, bsz), cols[col + 1]:cols[col + 2]]
            return are * re - aim * im + s_re, are * im + aim * re + s_im
        return step

    fstep = make_step(dec[0:1, :], dec[1:2, :], 0, 0)
    bstep = make_step(dec[2:3, :], dec[3:4, :], 2, 2 * LANES)
    lax.fori_loop(0, ncc + ncl, lambda c, s: fstep(c * bsz, s), (zero, zero))
    st = lax.fori_loop(0, ncc, lambda i, s: bstep((ncc - 1 - i) * bsz, s), (zero, zero))
    lax.fori_loop(0, ncl, lambda i, s: bstep((ncc + ncl - 1 - i) * bsz, s), st)

    w2 = w2_ref[0]
    dsk = dsk_ref[0]
    pt = pt_ref[0]
    for _, y_ref, c0, n, sl in blocks():
        y = z_scr[sl, 0:width] + _dot(h_scr[sl, :].astype(BF16), w2)
        y = jax.nn.gelu(y + dsk * u_scr[sl, :].astype(F32)).astype(BF16)
        yn = _dot(y, pt)
        for tt in range(t):
            slab = yn[:, tt * LANES:(tt + 1) * LANES].astype(BF16).reshape(n, bsz, LANES)
            y_ref[tt, c0:c0 + n] = y_ref[tt, c0:c0 + n] + slab


def _s5_lane_perm(t, hch):
    ng = LANES // hch
    pg = np.zeros((ng, t, LANES, t, hch), np.float32)
    for g in range(ng):
        for tt in range(t):
            pg[g, tt, g * hch + np.arange(hch), tt, np.arange(hch)] = 1.0
    return pg.reshape(ng, t * LANES, t * hch)


def _s5_core(hc, hl, tables, row_blk=512):
    w1, w2, dec, dsk = tables
    t, ncc, bsz, d = hc.shape
    ncl = hl.shape[1]
    width = w2.shape[2]
    hch = width // t
    ng = LANES // hch
    rows = (ncc + ncl) * bsz
    pg = _s5_lane_perm(t, hch)
    kern = functools.partial(_s5_kernel, bsz=bsz, ncc=ncc, ncl=ncl, width=width, row_blk=row_blk)
    gmap = lambda j, q: (j * ng + q, 0, 0)
    qmap = lambda j, q: (q, 0, 0)
    once = pl.Buffered(1)
    return pl.pallas_call(
        kern,
        grid=(d // LANES, ng),
        in_specs=[pl.BlockSpec((t, ncc, bsz, LANES), lambda j, q: (0, 0, 0, j), pipeline_mode=once),
                  pl.BlockSpec((t, ncl, bsz, LANES), lambda j, q: (0, 0, 0, j), pipeline_mode=once),
                  pl.BlockSpec((1,) + pg.shape[1:], qmap), pl.BlockSpec((1, t * hch, t * LANES), qmap),
                  pl.BlockSpec((1,) + w1.shape[1:], gmap), pl.BlockSpec((1,) + w2.shape[1:], gmap),
                  pl.BlockSpec((1,) + dec.shape[1:], gmap), pl.BlockSpec((1,) + dsk.shape[1:], gmap)],
        out_specs=[pl.BlockSpec((t, ncc, bsz, LANES), lambda j, q: (0, 0, 0, j), pipeline_mode=once),
                   pl.BlockSpec((t, ncl, bsz, LANES), lambda j, q: (0, 0, 0, j), pipeline_mode=once)],
        out_shape=[jax.ShapeDtypeStruct(hc.shape, BF16), jax.ShapeDtypeStruct(hl.shape, BF16)],
        scratch_shapes=[pltpu.VMEM((rows, w1.shape[2]), F32), pltpu.VMEM((rows, 4 * LANES), F32),
                        pltpu.VMEM((rows, width), BF16)],
        compiler_params=_params("parallel", "arbitrary"),
        name="s5_core",
    )(hc, hl, jnp.asarray(pg, dtype=BF16), jnp.asarray(pg.transpose(0, 2, 1), dtype=BF16), w1, w2, dec, dsk)


def _time_major(hm, bsz, t):
    n, d = hm.shape
    return jnp.transpose(hm.reshape(bsz, n // bsz // t, t, d), (2, 1, 0, 3))


def _token_major(y):
    t, nc, bsz, d = y.shape
    return jnp.transpose(y, (2, 1, 0, 3)).reshape(bsz * nc * t, d)


def _mixout_kernel(y_ref, w_ref, x_ref, mod_ref, o_ref, *, glu):
    r = _dot(y_ref[...], w_ref[...])
    if glu:
        half = r.shape[-1] // 2
        r = r[:, :half] * jax.nn.sigmoid(r[:, half:])
    o_ref[...] = x_ref[...] + mod_ref[0, 2:3, :] * r


def _mixout(y, w, x, mod, tile, tiles_per_mod, glu):
    n, d = x.shape
    k, nn = w.shape
    return pl.pallas_call(
        functools.partial(_mixout_kernel, glu=glu),
        grid=(n // tile,),
        in_specs=[pl.BlockSpec((tile, k), lambda i: (i, 0)),
                  pl.BlockSpec((k, nn), lambda i: (0, 0)),
                  pl.BlockSpec((tile, d), lambda i: (i, 0)),
                  pl.BlockSpec((1, 6, d), lambda i: (i // tiles_per_mod, 0, 0))],
        out_specs=pl.BlockSpec((tile, d), lambda i: (i, 0)),
        out_shape=jax.ShapeDtypeStruct((n, d), F32),
        compiler_params=_params("parallel"),
        name="mixer_out",
    )(y, w, x, mod)


def _mlp_kernel(x_ref, nw_ref, mod_ref, w1_ref, w2_ref, fw_ref, o_ref, *, ff_blk, final):
    x = x_ref[...]
    h = _norm_mod(x, nw_ref[...], mod_ref[0, 3:4, :], mod_ref[0, 4:5, :]).astype(BF16)
    acc = jnp.zeros(x.shape, F32)
    for f0 in range(0, w1_ref.shape[1], ff_blk):
        a = jnp.maximum(_dot(h, w1_ref[:, f0:f0 + ff_blk]), 0.0)
        acc = acc + _dot((a * a).astype(BF16), w2_ref[f0:f0 + ff_blk, :])
    y = x + mod_ref[0, 5:6, :] * acc
    if final:
        y = y * lax.rsqrt(jnp.mean(y * y, axis=-1, keepdims=True) + NORM_EPS) * fw_ref[...]
    o_ref[...] = y


def _mlp(x, nw, mod, w1, w2, fw, tile, tiles_per_mod, final):
    n, d = x.shape
    ff = w1.shape[1]
    return pl.pallas_call(
        functools.partial(_mlp_kernel, ff_blk=min(1024, ff), final=final),
        grid=(n // tile,),
        in_specs=[pl.BlockSpec((tile, d), lambda i: (i, 0)),
                  pl.BlockSpec((1, d), lambda i: (0, 0)),
                  pl.BlockSpec((1, 6, d), lambda i: (i // tiles_per_mod, 0, 0)),
                  pl.BlockSpec((d, ff), lambda i: (0, 0)),
                  pl.BlockSpec((ff, d), lambda i: (0, 0)),
                  pl.BlockSpec((1, d), lambda i: (0, 0))],
        out_specs=pl.BlockSpec((tile, d), lambda i: (i, 0)),
        out_shape=jax.ShapeDtypeStruct((n, d), F32),
        compiler_params=_params("parallel"),
        name="mlp",
    )(x, nw.reshape(1, d), mod, w1, w2, fw.reshape(1, d))


def _outmlp_kernel(y_ref, wo_ref, x_ref, nw_ref, mod_ref, w1_ref, w2_ref, fw_ref, o_ref, *, ff_blk, glu, final):
    r = _dot(y_ref[...], wo_ref[...])
    if glu:
        half = r.shape[-1] // 2
        r = r[:, :half] * jax.nn.sigmoid(r[:, half:])
    x = x_ref[...] + mod_ref[0, 2:3, :] * r
    h = _norm_mod(x, nw_ref[...], mod_ref[0, 3:4, :], mod_ref[0, 4:5, :]).astype(BF16)
    acc = jnp.zeros(x.shape, F32)
    for f0 in range(0, w1_ref.shape[1], ff_blk):
        a = jnp.maximum(_dot(h, w1_ref[:, f0:f0 + ff_blk]), 0.0)
        acc = acc + _dot((a * a).astype(BF16), w2_ref[f0:f0 + ff_blk, :])
    y = x + mod_ref[0, 5:6, :] * acc
    if final:
        y = y * lax.rsqrt(jnp.mean(y * y, axis=-1, keepdims=True) + NORM_EPS) * fw_ref[...]
    o_ref[...] = y


def _outmlp(y, wo, x, nw, mod, w1, w2, fw, tile, tiles_per_mod, glu, final):
    n, d = x.shape
    k = y.shape[1]
    ff = w1.shape[1]
    once = pl.Buffered(1)
    return pl.pallas_call(
        functools.partial(_outmlp_kernel, ff_blk=min(1024, ff), glu=glu, final=final),
        grid=(n // tile,),
        in_specs=[pl.BlockSpec((tile, k), lambda i: (i, 0)),
                  pl.BlockSpec(wo.shape, lambda i: (0, 0), pipeline_mode=once),
                  pl.BlockSpec((tile, d), lambda i: (i, 0)),
                  pl.BlockSpec((1, d), lambda i: (0, 0)),
                  pl.BlockSpec((1, 6, d), lambda i: (i // tiles_per_mod, 0, 0)),
                  pl.BlockSpec((d, ff), lambda i: (0, 0), pipeline_mode=once),
                  pl.BlockSpec((ff, d), lambda i: (0, 0), pipeline_mode=once),
                  pl.BlockSpec((1, d), lambda i: (0, 0))],
        out_specs=pl.BlockSpec((tile, d), lambda i: (i, 0)),
        out_shape=jax.ShapeDtypeStruct((n, d), F32),
        compiler_params=_params("parallel"),
        name="out_mlp",
    )(y, wo, x, nw.reshape(1, d), mod, w1, w2, fw.reshape(1, d))


def _gdn_in_kernel(x_ref, nw_ref, mod_ref, w_ref, wg_ref, cw_ref, gp_ref, p_ref, g_ref, gt_ref, acc_scr,
                   *, period, col0, qk, vd, tn, hv, hdim):
    tb = x_ref.shape[0]
    h = _norm_mod(x_ref[...], nw_ref[...], mod_ref[0, 0:1, :], mod_ref[0, 1:2, :])
    hb = h.astype(BF16)
    gl = _dot(h, wg_ref[...], HIGHEST)
    lane = lax.broadcasted_iota(jnp.int32, (1, LANES), 1)
    beta = jax.nn.sigmoid(gl)
    t = gl + gp_ref[1:2, :]
    g = gp_ref[0:1, :] * (jnp.maximum(t, 0.0) + jnp.log1p(jnp.exp(-jnp.abs(t))))
    r = lax.broadcasted_iota(jnp.int32, (tb, tb), 0)
    c = lax.broadcasted_iota(jnp.int32, (tb, tb), 1)
    same = _floor_pow2(r, GDN_CHUNK) == _floor_pow2(c, GDN_CHUNK)
    gcf = _dot(jnp.where(same & (c <= r), 1.0, 0.0), g, HIGHEST)
    gcb = _dot(jnp.where(same & (c >= r), 1.0, 0.0), g, HIGHEST)
    out = jnp.where(lane < 2 * hv, beta, jnp.where(lane < 3 * hv, gcf, jnp.where(lane < 4 * hv, gcb, 0.0)))
    g_ref[...] = out
    for cc in range(tb // GDN_CHUNK):
        gt_ref[cc] = out[cc * GDN_CHUNK:(cc + 1) * GDN_CHUNK, :].T[2 * hv:4 * hv, :]

    assert tb % period == 0
    row = lax.broadcasted_iota(jnp.int32, (period, 1), 0)
    edge = [(tap, ((row + tap - 2 >= 0) & (row + tap - 2 < period)).astype(F32)) for tap in (0, 1, 3, 4)]

    for bi, o0 in enumerate(range(0, p_ref.shape[1], tn)):
        c0 = col0 + o0
        acc = _dot(hb, w_ref[:, c0:c0 + tn])
        if qk <= c0 < qk + vd:
            p_ref[:, o0:o0 + tn] = acc.astype(BF16)
            continue
        acc_ref = acc_scr.at[bi % 2]
        acc_ref[...] = acc
        scale = hdim ** -0.5 if c0 < qk else (1.0 if c0 < 2 * qk + vd else None)
        for h0 in range(0, tn, hdim):
            cw = cw_ref[:, c0 + h0:c0 + h0 + hdim]
            taps = [(tap, ok * cw[tap:tap + 1, :]) for tap, ok in edge]
            for r0 in range(0, tb, period):
                a = acc_ref[r0:r0 + period, h0:h0 + hdim]
                y = a * cw[2:3, :]
                for tap, wt in taps:
                    y = y + pltpu.roll(a, (period - (tap - 2)) % period, 0) * wt
                y = _silu(y)
                if scale is not None:
                    y = y * (lax.rsqrt(jnp.sum(y * y, axis=-1, keepdims=True) + NORM_EPS) * scale)
                p_ref[r0:r0 + period, o0 + h0:o0 + h0 + hdim] = y.astype(BF16)


def _gdn_in(x, nw, mod, w_main, w_gate, cw, gp, tile, tiles_per_mod, period, col0, tn, qk, vd, hv, hdim):
    n, d = x.shape
    n_main = w_main.shape[1]
    kern = functools.partial(_gdn_in_kernel, period=period, col0=col0, qk=qk, vd=vd, tn=tn, hv=hv, hdim=hdim)
    return pl.pallas_call(
        kern,
        grid=(n // tile,),
        in_specs=[pl.BlockSpec((tile, d), lambda i: (i, 0)),
                  pl.BlockSpec((1, d), lambda i: (0, 0)),
                  pl.BlockSpec((1, 6, d), lambda i: (i // tiles_per_mod, 0, 0)),
                  pl.BlockSpec((d, n_main), lambda i: (0, 0)),
                  pl.BlockSpec((d, LANES), lambda i: (0, 0)),
                  pl.BlockSpec((8, n_main), lambda i: (0, 0)),
                  pl.BlockSpec((2, LANES), lambda i: (0, 0))],
        out_specs=[pl.BlockSpec((tile, n_main - col0), lambda i: (i, 0)),
                   pl.BlockSpec((tile, LANES), lambda i: (i, 0)),
                   pl.BlockSpec((tile // GDN_CHUNK, 2 * hv, GDN_CHUNK), lambda i: (i, 0, 0))],
        out_shape=[jax.ShapeDtypeStruct((n, n_main - col0), BF16),
                   jax.ShapeDtypeStruct((n, LANES), F32),
                   jax.ShapeDtypeStruct((n // GDN_CHUNK, 2 * hv, GDN_CHUNK), F32)],
        scratch_shapes=[pltpu.VMEM((2, tile, tn), F32)],
        compiler_params=_params("parallel"),
        name="gdn_in",
    )(x, nw.reshape(1, d), mod, w_main, w_gate, cw, gp)


def _gdn_masks(ch, rep):
    nco = 2 * rep
    r, c = np.meshgrid(np.arange(ch), np.arange(ch), indexing="ij")
    per_dir = []
    for lower in (True, False):
        m = [(c <= r) if lower else (c >= r), (c < r) if lower else (c > r)]
        b = 1
        while b < ch:
            same = (r // (2 * b)) == (c // (2 * b))
            hi_r, hi_c = (r % (2 * b)) >= b, (c % (2 * b)) >= b
            m.append(same & ((hi_r & ~hi_c) if lower else (~hi_r & hi_c)))
            b *= 2
        per_dir.append(np.stack(m))
    cm = np.concatenate([per_dir[j // rep] for j in range(nco)], axis=2).astype(np.float32)
    blk = np.arange(nco * ch) // ch
    return cm, (blk[:, None] == blk[None, :]).astype(np.float32)


def _gdn_kernel(q_ref, k_ref, v_ref, z_ref, kc_ref, vc_ref, g_ref, gt_ref, gc_ref, gct_ref, ow_ref, cm_ref, lv_ref, bd_ref,
                o_ref, of_s, ob_s, st_s, uw_s, in_s, kt_s, *, hv, khb, rep, hdim, ncc, ncl, unroll):
    ch = GDN_CHUNK
    nco = 2 * rep
    nlvl = lv_ref.shape[0]
    kh0 = pl.program_id(1) * khb
    lane = lax.broadcasted_iota(jnp.int32, (1, LANES), 1)
    lcat = lax.broadcasted_iota(jnp.int32, (1, nco * ch), 1)

    def lanes_of(khl, j):
        head = (kh0 + khl) * rep + j % rep
        return (j // rep) * hv + head, (2 + j // rep) * hv + head

    def column(gt, idx):
        return jnp.sum(jnp.where(lane == idx, gt, 0.0), axis=1, keepdims=True)

    def spread(cols):
        out = jnp.broadcast_to(cols[nco - 1], (ch, nco * ch))
        for j in range(nco - 2, -1, -1):
            out = jnp.where(lcat < (j + 1) * ch, cols[j], out)
        return out

    def block_diag(xb):
        return jnp.concatenate([xb] * nco, axis=0) * bd_ref[...]

    def prepare(insts):
        st = []
        for khl, k_r, q_r, v_r, g_r, gt_r, c, slot in insts:
            rows = pl.ds(pl.multiple_of(c * ch, ch), ch)
            kb = k_r[rows, khl * hdim:(khl + 1) * hdim]
            kcat = jnp.concatenate([kb] * nco, axis=0)
            if q_r is None:
                kkc, qkc = _dot_nt(kb, kcat), None
            else:
                prod = _dot_nt(jnp.concatenate([kb, q_r[rows, khl * hdim:(khl + 1) * hdim]], axis=0), kcat)
                kkc, qkc = prod[:ch], prod[ch:]
            gtile = g_r[rows, :]
            betas, gcols, grows = [], [], []
            for j in range(nco):
                bl, gl = lanes_of(khl, j)
                betas.append(column(gtile, bl))
                gcols.append(column(gtile, gl))
                grows.append(gt_r[c, pl.ds(gl - 2 * hv, 1), :])
            dlog = spread(gcols) - jnp.concatenate(grows, axis=1)
            decay = jnp.exp(jnp.where(cm_ref[0] > 0.0, dlog, -jnp.inf))
            ac = spread(betas) * kkc * decay * cm_ref[1]
            xb = ((cm_ref[0] - cm_ref[1]) - ac * cm_ref[2]).astype(BF16)
            st.append([rows, kb, qkc, betas, gcols, decay, ac.astype(BF16), xb])
        for lvl in range(1, nlvl):
            ps = [_dot(s[6] * lv_ref[lvl], block_diag(s[7])) for s in st]
            for s, p in zip(st, ps):
                s[7] = s[7] - _dot(s[7], block_diag(p.astype(BF16))).astype(BF16)
        outs = []
        for (khl, k_r, q_r, v_r, g_r, gt_r, c, slot), (rows, kb, qkc, betas, gcols, decay, ac, xc) in zip(insts, st):
            kf = kb.astype(F32)
            rst = []
            for j in range(nco):
                v0 = (khl * rep + j % rep) * hdim
                vf = v_r[rows, v0:v0 + hdim].astype(F32)
                rst.append(jnp.concatenate([vf * betas[j], kf * (betas[j] * jnp.exp(gcols[j]))], axis=1))
            uw = _dot(block_diag(xc), jnp.concatenate(rst, axis=0).astype(BF16))
            intra = None if qkc is None else (qkc * decay * cm_ref[0]).astype(BF16)
            outs.append((uw, kf.T.astype(BF16), intra))
        for (khl, k_r, q_r, v_r, g_r, gt_r, c, slot), (uw, kt, intra) in zip(insts, outs):
            for j in range(nco):
                uw_s[slot, khl * nco + j] = uw[j * ch:(j + 1) * ch, :].astype(BF16)
                if intra is not None:
                    in_s[c, khl * nco + j] = intra[:, j * ch:(j + 1) * ch]
            kt_s[slot, khl] = kt

    def advance(items):
        mid = []
        for khl, j, q_r, g_r, gt_r, c, slot in items:
            rows = pl.ds(pl.multiple_of(c * ch, ch), ch)
            _, gl = lanes_of(khl, j)
            gcol = column(g_r[rows, :], gl)
            grow = gt_r[c, pl.ds(gl - 2 * hv, 1), :]
            glast = grow[:, ch - 1:ch] if j < rep else grow[:, 0:1]
            sidx = khl * nco + j
            s = st_s[sidx]
            lhs = uw_s[slot, sidx, :, hdim:]
            if q_r is not None:
                qg = q_r[rows, khl * hdim:(khl + 1) * hdim].astype(F32) * jnp.exp(gcol)
                lhs = jnp.concatenate([lhs, qg.astype(BF16)], axis=0)
            mid.append((rows, gcol, glast, sidx, s, _dot(lhs, s.astype(BF16))))
        res = []
        for (khl, j, q_r, g_r, gt_r, c, slot), (rows, gcol, glast, sidx, s, ws) in zip(items, mid):
            vnew = uw_s[slot, sidx, :, :hdim].astype(F32) - ws[:ch]
            snew = s * jnp.exp(glast) + _dot(kt_s[slot, khl], (vnew * jnp.exp(glast - gcol)).astype(BF16))
            o = None if q_r is None else ws[ch:] + _dot(in_s[c, sidx], vnew.astype(BF16))
            res.append((snew, o))
        for (khl, j, q_r, g_r, gt_r, c, slot), (rows, gcol, glast, sidx, s, ws), (snew, o) in zip(items, mid, res):
            st_s[sidx] = snew
            if o is not None:
                v0 = (khl * rep + j % rep) * hdim
                (of_s if j < rep else ob_s)[rows, v0:v0 + hdim] = o.astype(BF16)

    st_s[...] = jnp.zeros(st_s.shape, F32)
    uc = min(unroll, ncc)

    def prep_ctx(i, carry):
        prepare([(khl, kc_ref, None, vc_ref, gc_ref, gct_ref, i * uc + t, i * uc + t)
                 for t in range(uc) for khl in range(khb)])
        return carry

    def prep_lat(i, carry):
        prepare([(khl, k_ref, q_ref, v_ref, g_ref, gt_ref, i * unroll + t, ncc + i * unroll + t)
                 for t in range(unroll) for khl in range(khb)])
        return carry

    def adv_ctx(i, carry):
        advance([(khl, j, None, gc_ref, gct_ref, i if j < rep else ncc - 1 - i, i if j < rep else ncc - 1 - i)
                 for khl in range(khb) for j in range(nco)])
        return carry

    def adv_lat(i, carry):
        advance([(khl, j, q_ref, g_ref, gt_ref, i if j < rep else ncl - 1 - i,
                  ncc + (i if j < rep else ncl - 1 - i)) for khl in range(khb) for j in range(nco)])
        return carry

    lax.fori_loop(0, ncc // uc, prep_ctx, 0)
    lax.fori_loop(0, ncl // unroll, prep_lat, 0)
    lax.fori_loop(0, ncc, adv_ctx, 0)
    lax.fori_loop(0, ncl, adv_lat, 0)

    blk = min(256, ncl * ch)
    ow = ow_ref[...]
    for r0 in range(0, ncl * ch, blk):
        for v0 in range(0, khb * rep * hdim, hdim):
            o = of_s[r0:r0 + blk, v0:v0 + hdim].astype(F32) + ob_s[r0:r0 + blk, v0:v0 + hdim].astype(F32)
            o = o * lax.rsqrt(jnp.mean(o * o, axis=-1, keepdims=True) + NORM_EPS) * ow
            o_ref[r0:r0 + blk, v0:v0 + hdim] = (o * _silu(z_ref[r0:r0 + blk, v0:v0 + hdim].astype(F32))).astype(BF16)


def _gdn_core(p_lat, p_ctx, g_lat, gt_lat, g_ctx, gt_ctx, onorm_w, bsz, hk, hv, hdim, khb=GDN_KEY_HEADS_PER_STEP):
    l = p_lat.shape[0] // bsz
    lc = p_ctx.shape[0] // bsz
    ch = GDN_CHUNK
    ncc, ncl = lc // ch, l // ch
    rep = hv // hk
    nco = 2 * rep
    kw, vw = khb * hdim, khb * rep * hdim
    qk, vd = hk * hdim, hv * hdim
    unroll = min(GDN_PREP_UNROLL, ncl)
    assert hk % khb == 0 and ncl % unroll == 0 and ncc % min(unroll, ncc) == 0
    assert qk % vw == 0 and vd % vw == 0
    kern = functools.partial(_gdn_kernel, hv=hv, khb=khb, rep=rep, hdim=hdim, ncc=ncc, ncl=ncl, unroll=unroll)
    kcol, vcol, zcol = (qk + vd) // kw, (2 * qk + vd) // vw, qk // vw
    cm, bd = _gdn_masks(ch, rep)
    return pl.pallas_call(
        kern,
        grid=(bsz, hk // khb),
        in_specs=[pl.BlockSpec((l, kw), lambda b, h: (b, h)),
                  pl.BlockSpec((l, kw), lambda b, h: (b, kcol + h)),
                  pl.BlockSpec((l, vw), lambda b, h: (b, vcol + h)),
                  pl.BlockSpec((l, vw), lambda b, h: (b, zcol + h)),
                  pl.BlockSpec((lc, kw), lambda b, h: (b, h)),
                  pl.BlockSpec((lc, vw), lambda b, h: (b, qk // vw + h)),
                  pl.BlockSpec((l, LANES), lambda b, h: (b, 0)),
                  pl.BlockSpec((ncl, 2 * hv, ch), lambda b, h: (b, 0, 0)),
                  pl.BlockSpec((lc, LANES), lambda b, h: (b, 0)),
                  pl.BlockSpec((ncc, 2 * hv, ch), lambda b, h: (b, 0, 0)),
                  pl.BlockSpec((1, hdim), lambda b, h: (0, 0)),
                  pl.BlockSpec(cm[:3].shape, lambda b, h: (0, 0, 0)),
                  pl.BlockSpec(cm[2:].shape, lambda b, h: (0, 0, 0)),
                  pl.BlockSpec(bd.shape, lambda b, h: (0, 0))],
        out_specs=pl.BlockSpec((l, vw), lambda b, h: (b, h)),
        out_shape=jax.ShapeDtypeStruct((bsz * l, vd), BF16),
        scratch_shapes=[pltpu.VMEM((l, vw), BF16), pltpu.VMEM((l, vw), BF16),
                        pltpu.VMEM((khb * nco, hdim, hdim), F32),
                        pltpu.VMEM((ncc + ncl, khb * nco, ch, 2 * hdim), BF16),
                        pltpu.VMEM((ncl, khb * nco, ch, ch), BF16),
                        pltpu.VMEM((ncc + ncl, khb, hdim, ch), BF16)],
        compiler_params=_params("parallel", "arbitrary"),
        name="gdn_core",
    )(p_lat, p_lat, p_lat, p_lat, p_ctx, p_ctx, g_lat, gt_lat, g_ctx, gt_ctx, onorm_w.reshape(1, hdim),
      jnp.asarray(cm[:3]), jnp.asarray(cm[2:], dtype=BF16), jnp.asarray(bd, dtype=BF16))


def kernel(x, c, ctx, c_ctx, ada_w, ada_b, norm1_w, norm2_w, mlp_w1, mlp_w2, s5_lam_re, s5_lam_im, s5_log_dt, s5_b_re, s5_b_im, s5_c_re, s5_c_im, s5_d, s5_w_glu, gdn_w_in, gdn_conv_w, gdn_a_log, gdn_dt_bias, gdn_onorm_w, gdn_w_out, final_norm_w):
    bsz, l, d = x.shape
    lc = ctx.shape[1]
    depth = ada_w.shape[0]
    assert depth == 2 and s5_lam_re.shape[0] == 1 and gdn_w_in.shape[0] == 1
    hv = gdn_a_log.shape[2]
    hdim = gdn_onorm_w.shape[1]
    vd = gdn_w_out.shape[1]
    qk = (gdn_conv_w.shape[2] - vd) // 2
    hk = qk // hdim
    assert l % GRID_W == 0 and l % GDN_CHUNK == 0 and lc % GDN_CHUNK == 0
    assert l % S5_CHUNK == 0 and lc % S5_CHUNK == 0 and 4 * hv <= LANES

    tb_l, tb_c = min(256, l), lc
    tm_l, tm_c = min(512, l), min(512, bsz * lc)
    tx_l = min(1024, l)
    xl = x.reshape(bsz * l, d)
    xc = ctx.reshape(bsz * lc, d)

    rows = -(-(bsz + 1) // 8) * 8
    cvec = jnp.zeros((rows, d), F32).at[:bsz].set(c).at[bsz].set(c_ctx)
    mod = _ada_mod(cvec, ada_w, ada_b).reshape(depth, rows, 6, d)
    big = 1 << 30

    mod_l, mod_c = mod[0, :bsz], mod[0, bsz:bsz + 1]
    hl = _prenorm(xl, norm1_w[0], mod_l, tx_l, l // tx_l)
    hc = _prenorm(xc, norm1_w[0], mod_c, tb_c, big)
    tables = _s5_tables(s5_lam_re[0], s5_lam_im[0], s5_log_dt[0], s5_b_re[0], s5_b_im[0],
                        s5_c_re[0], s5_c_im[0], s5_d[0], S5_CHUNK)
    yc, yl = _s5_core(_time_major(hc, bsz, S5_CHUNK), _time_major(hl, bsz, S5_CHUNK), tables)
    w_glu = s5_w_glu[0].astype(BF16)
    w1, w2 = mlp_w1[0].astype(BF16), mlp_w2[0].astype(BF16)
    xl = _outmlp(_token_major(yl), w_glu, xl, norm2_w[0], mod_l, w1, w2, final_norm_w, tm_l, l // tm_l, True, False)
    xc = _mixout(_token_major(yc), w_glu, xc, mod_c, tm_c, big, True)
    xc = _mlp(xc, norm2_w[0], mod_c, w1, w2, final_norm_w, tm_c, big, False)

    mod_l, mod_c = mod[1, :bsz], mod[1, bsz:bsz + 1]
    w_in = gdn_w_in[0]
    n_main = 2 * qk + 2 * vd
    w_main = w_in[:, :n_main].astype(BF16)
    w_gate = jnp.pad(w_in[:, n_main:], ((0, 0), (0, LANES - 4 * hv)))
    conv_w = gdn_conv_w[0].astype(F32)
    cw = jnp.concatenate([conv_w[:, :qk], jnp.zeros((conv_w.shape[0], vd), F32), conv_w[:, qk:]], axis=1)
    cw = jnp.pad(cw, ((0, 8 - cw.shape[0]), (0, 0)))
    neg_a = jnp.pad(-jnp.exp(gdn_a_log[0].astype(F32)).reshape(-1), (2 * hv, LANES - 4 * hv))
    dtb = jnp.pad(gdn_dt_bias[0].astype(F32).reshape(-1), (2 * hv, LANES - 4 * hv))
    gp = jnp.stack([neg_a, dtb])
    tn = min(512, qk)
    p_l, g_l, gt_l = _gdn_in(xl, norm1_w[1], mod_l, w_main, w_gate, cw, gp, tb_l, l // tb_l, GRID_W,
                             0, tn, qk, vd, hv, hdim)
    p_c, g_c, gt_c = _gdn_in(xc, norm1_w[1], mod_c, w_main, w_gate, cw, gp, tb_c, big, lc,
                             qk + vd, tn, qk, vd, hv, hdim)
    gated = _gdn_core(p_l, p_c, g_l, gt_l, g_c, gt_c, gdn_onorm_w[0].astype(F32), bsz, hk, hv, hdim)
    out = _outmlp(gated, gdn_w_out[0].astype(BF16), xl, norm2_w[1], mod_l, mlp_w1[1].astype(BF16),
                  mlp_w2[1].astype(BF16), final_norm_w, tm_l, l // tm_l, False, True)
    return out.reshape(bsz, l, d)
```
